```python
import jax, jax.numpy as jnp
from jax import lax
import numpy as np

D_MODEL = 4096
BATCH = 4
SEQ = 2048
DEPTH = 2
DEC_BATCH = 128
DEC_SEQ = 8
PAST_LEN = 16384
PAGE_SIZE = 128

A_GROUPS = 8
A_GROUP_DIM = 256
A_WIDTH = A_GROUPS * A_GROUP_DIM
CHUNK = 128
N_HEADS = 16
HEAD_DIM = 128
B_WIDTH = N_HEADS * HEAD_DIM
QKV_DIM = 3 * B_WIDTH
CONV_W = 4
DELTA_CHUNK = 64
D_FF = 14336
N_EXPERTS = 8
TOP_K = 2
N_DENSE = (DEPTH + 1) // 2
N_MOE = DEPTH // 2
IN_COLS = 2 * A_WIDTH + QKV_DIM + B_WIDTH + 2 * N_HEADS + 2 * D_MODEL
EPS = 1e-6

kernel_name = "hybrid_gmlp_gdn_moe_decode_step"


def rmsnorm(x, w):
    x32 = x.astype(jnp.float32)
    y = x32 * lax.rsqrt(jnp.mean(x32 * x32, axis=-1, keepdims=True) + EPS)
    return (y * w.astype(jnp.float32)).astype(x.dtype)


def l2norm(x):
    x32 = x.astype(jnp.float32)
    return x32 * lax.rsqrt(jnp.sum(x32 * x32, axis=-1, keepdims=True) + EPS)


def causal_conv(x_in, buf, conv_w):
    L = x_in.shape[1]
    xp = jnp.concatenate([buf.astype(x_in.dtype), x_in], axis=1)
    out = xp[:, 0:L] * conv_w[0]
    for j in range(1, CONV_W):
        out = out + xp[:, j:j + L] * conv_w[j]
    return out, xp[:, L:]


def chunk_spatial_gating(u, v, w_s, b_s):
    B, L, _ = v.shape
    n_chunks = -(-L // CHUNK)
    Lp = n_chunks * CHUNK
    vp = jnp.pad(v, ((0, 0), (0, Lp - L), (0, 0))).reshape(B, n_chunks, CHUNK, A_GROUPS, A_GROUP_DIM)
    causal = jnp.tril(jnp.ones((CHUNK, CHUNK), dtype=bool))
    w_masked = jnp.where(causal, w_s, 0.0).astype(v.dtype)
    s = jnp.einsum('gts,bcsgd->bctgd', w_masked, vp) + b_s.T[None, None, :, :, None].astype(v.dtype)
    return u * s.reshape(B, Lp, A_WIDTH)[:, :L]


def gated_delta_rule(q, k, v, g, beta, S0):
    B, L, H, DK = q.shape
    C = min(DELTA_CHUNK, L)
    n = -(-L // C)
    Lp = n * C

    def blocks(t):
        t = jnp.pad(t.astype(jnp.float32), [(0, 0), (0, Lp - L)] + [(0, 0)] * (t.ndim - 2))
        t = jnp.swapaxes(t, 1, 2)
        return t.reshape((B, H, n, C) + t.shape[3:])

    q, k, v, g, beta = (blocks(t) for t in (q, k, v, g, beta))
    gc = jnp.cumsum(g, axis=-1)
    i = jnp.arange(C)
    incl = i[:, None] >= i[None, :]
    strict = i[:, None] > i[None, :]
    decay = jnp.where(incl, jnp.exp(jnp.where(incl, gc[..., :, None] - gc[..., None, :], 0.0)), 0.0)
    kb = k * beta[..., None]
    a_mat = jnp.where(strict, jnp.einsum('bhnid,bhnjd->bhnij', kb, k), 0.0) * decay
    eye = jnp.eye(C, dtype=jnp.float32)
    t_mat = lax.linalg.triangular_solve(eye + a_mat, jnp.broadcast_to(eye, a_mat.shape),
                                        left_side=True, lower=True, unit_diagonal=True)
    u_blk = jnp.einsum('bhnij,bhnjd->bhnid', t_mat, v * beta[..., None])
    w_blk = jnp.einsum('bhnij,bhnjd->bhnid', t_mat, kb * jnp.exp(gc)[..., None])
    qk = jnp.einsum('bhnid,bhnjd->bhnij', q, k) * decay
    q_dec = q * jnp.exp(gc)[..., None]
    g_last = gc[..., -1]
    k_dec = k * jnp.exp(g_last[..., None] - gc)[..., None]

    def step(S, xs):
        q_c, k_c, w_c, u_c, qk_c, gl = xs
        v_new = u_c - jnp.einsum('bhcd,bhde->bhce', w_c, S)
        o = jnp.einsum('bhcd,bhde->bhce', q_c, S) + jnp.einsum('bhij,bhje->bhie', qk_c, v_new)
        S = S * jnp.exp(gl)[..., None, None] + jnp.einsum('bhcd,bhce->bhde', k_c, v_new)
        return S, o

    xs = tuple(jnp.moveaxis(t, 2, 0) for t in (q_dec, k_dec, w_blk, u_blk, qk, g_last))
    S, o = lax.scan(step, S0.astype(jnp.float32), xs)
    o = jnp.moveaxis(o, 0, 2).reshape(B, H, Lp, -1)[:, :, :L]
    return jnp.swapaxes(o, 1, 2), S


def mixer(xn, conv_buf, S0, w_in, conv_w, a_log, dt_bias, o_norm_w, sgu_norm_w, w_s, b_s, w_pa, w_pb, w_o):
    B, L, _ = xn.shape
    proj = jnp.einsum('bld,dc->blc', xn, w_in)
    c1 = A_WIDTH
    c2 = 2 * A_WIDTH
    c3 = c2 + QKV_DIM
    c4 = c3 + B_WIDTH
    c5 = c4 + N_HEADS
    c6 = c5 + N_HEADS
    c7 = c6 + D_MODEL
    u, v, qkv, z, beta_logit, decay_logit, gate_a, gate_b = jnp.split(proj, [c1, c2, c3, c4, c5, c6, c7], axis=-1)
    u = jax.nn.gelu(u)
    v = rmsnorm(jax.nn.gelu(v), sgu_norm_w)
    a_out = chunk_spatial_gating(u, v, w_s, b_s)
    qkv, new_buf = causal_conv(qkv, conv_buf, conv_w)
    qkv = jax.nn.silu(qkv)
    q, k, vd = jnp.split(qkv, 3, axis=-1)
    q = l2norm(q.reshape(B, L, N_HEADS, HEAD_DIM)) * HEAD_DIM ** -0.5
    k = l2norm(k.reshape(B, L, N_HEADS, HEAD_DIM))
    vd = vd.reshape(B, L, N_HEADS, HEAD_DIM)
    beta = jax.nn.sigmoid(beta_logit.astype(jnp.float32))
    g = -jnp.exp(a_log.astype(jnp.float32)) * jax.nn.softplus(decay_logit.astype(jnp.float32) + dt_bias.astype(jnp.float32))
    o, S = gated_delta_rule(q, k, vd, g, beta, S0)
    o = rmsnorm(o, o_norm_w).astype(xn.dtype) * jax.nn.silu(z.reshape(B, L, N_HEADS, HEAD_DIM))
    b_out = o.reshape(B, L, B_WIDTH)
    merged = jax.nn.sigmoid(gate_a) * (a_out @ w_pa) + jax.nn.sigmoid(gate_b) * (b_out @ w_pb)
    return merged @ w_o, new_buf, S, v


def swiglu(x, wg, wu, wd):
    return (jax.nn.silu(x @ wg) * (x @ wu)) @ wd


def moe_ffn(x, router, wg, wu, wd):
    logits = jnp.einsum('bld,de->ble', x, router).astype(jnp.float32)
    top_v, top_i = lax.top_k(logits, TOP_K)
    gates = jax.nn.softmax(top_v, axis=-1)
    combine = jnp.sum(jax.nn.one_hot(top_i, N_EXPERTS, dtype=jnp.float32) * gates[..., None], axis=-2).astype(x.dtype)
    y = jnp.zeros_like(x)
    for e in range(N_EXPERTS):
        y = y + combine[..., e:e + 1] * swiglu(x, wg[e], wu[e], wd[e])
    return y


def trunk(x, conv_state, delta_state, p):
    new_conv, new_delta, chunk_v = [], [], []
    for l in range(DEPTH):
        xn = rmsnorm(x, p['norm_mix'][l])
        mix, cb, S, v_rows = mixer(xn, conv_state[l], delta_state[l], p['w_in'][l], p['conv_w'][l],
                                   p['a_log'][l], p['dt_bias'][l], p['o_norm_w'][l], p['sgu_norm_w'][l],
                                   p['w_spatial'][l], p['b_spatial'][l], p['w_proj_a'][l], p['w_proj_b'][l],
                                   p['w_out'][l])
        x = x + mix
        new_conv.append(cb)
        new_delta.append(S)
        chunk_v.append(v_rows)
        xn = rmsnorm(x, p['norm_ffn'][l])
        if l % 2 == 0:
            j = l // 2
            x = x + swiglu(xn, p['ffn_w_gate'][j], p['ffn_w_up'][j], p['ffn_w_down'][j])
        else:
            j = l // 2
            x = x + moe_ffn(xn, p['router'][j], p['moe_w_gate'][j], p['moe_w_up'][j], p['moe_w_down'][j])
    y = rmsnorm(x, p['norm_final'])
    return y, jnp.stack(new_conv), jnp.stack(new_delta), jnp.stack(chunk_v)


def setup_inputs(seed: int = 0) -> dict:
    key = jax.random.key(seed)
    ks = jax.random.split(key, 26)

    def nrm(k, shape, scale):
        return jax.random.normal(k, shape, jnp.float32) * scale

    dt = jnp.exp(jax.random.uniform(ks[10], (DEPTH, N_HEADS), jnp.float32, np.log(1e-3), np.log(1e-1)))
    return {
        'x_prompt': nrm(ks[0], (BATCH, SEQ, D_MODEL), 1.0),
        'x_sample': nrm(ks[1], (DEC_BATCH, DEC_SEQ, D_MODEL), 1.0),
        'state_delta': nrm(ks[2], (DEPTH, DEC_BATCH, N_HEADS, HEAD_DIM, HEAD_DIM), 0.1),
        'state_conv': nrm(ks[3], (DEPTH, DEC_BATCH, CONV_W - 1, QKV_DIM), 1.0),
        'norm_mix': 1.0 + nrm(ks[4], (DEPTH, D_MODEL), 0.02),
        'norm_ffn': 1.0 + nrm(ks[5], (DEPTH, D_MODEL), 0.02),
        'norm_final': 1.0 + nrm(ks[6], (D_MODEL,), 0.02),
        'w_in': nrm(ks[7], (DEPTH, D_MODEL, IN_COLS), D_MODEL ** -0.5),
        'conv_w': nrm(ks[8], (DEPTH, CONV_W, QKV_DIM), CONV_W ** -0.5),
        'a_log': jnp.log(jax.random.uniform(ks[9], (DEPTH, N_HEADS), jnp.float32, 1.0, 16.0)),
        'dt_bias': dt + jnp.log(-jnp.expm1(-dt)),
        'o_norm_w': 1.0 + nrm(ks[11], (DEPTH, HEAD_DIM), 0.02),
        'sgu_norm_w': 1.0 + nrm(ks[12], (DEPTH, A_WIDTH), 0.02),
        'w_spatial': nrm(ks[13], (DEPTH, A_GROUPS, CHUNK, CHUNK), CHUNK ** -0.5),
        'b_spatial': 1.0 + nrm(ks[14], (DEPTH, A_GROUPS, CHUNK), 0.02),
        'w_proj_a': nrm(ks[15], (DEPTH, A_WIDTH, D_MODEL), A_WIDTH ** -0.5),
        'w_proj_b': nrm(ks[16], (DEPTH, B_WIDTH, D_MODEL), B_WIDTH ** -0.5),
        'w_out': nrm(ks[17], (DEPTH, D_MODEL, D_MODEL), D_MODEL ** -0.5),
        'ffn_w_gate': nrm(ks[18], (N_DENSE, D_MODEL, D_FF), D_MODEL ** -0.5),
        'ffn_w_up': nrm(ks[19], (N_DENSE, D_MODEL, D_FF), D_MODEL ** -0.5),
        'ffn_w_down': nrm(ks[20], (N_DENSE, D_FF, D_MODEL), D_FF ** -0.5),
        'router': nrm(ks[21], (N_MOE, D_MODEL, N_EXPERTS), D_MODEL ** -0.5),
        'moe_w_gate': nrm(ks[22], (N_MOE, N_EXPERTS, D_MODEL, D_FF), D_MODEL ** -0.5),
        'moe_w_up': nrm(ks[23], (N_MOE, N_EXPERTS, D_MODEL, D_FF), D_MODEL ** -0.5),
        'moe_w_down': nrm(ks[24], (N_MOE, N_EXPERTS, D_FF, D_MODEL), D_FF ** -0.5),
    }


def reference(x_prompt, x_sample, state_delta, state_conv, norm_mix, norm_ffn, norm_final, w_in, conv_w,
              a_log, dt_bias, o_norm_w, sgu_norm_w, w_spatial, b_spatial, w_proj_a, w_proj_b, w_out,
              ffn_w_gate, ffn_w_up, ffn_w_down, router, moe_w_gate, moe_w_up, moe_w_down):
    p = {'norm_mix': norm_mix, 'norm_ffn': norm_ffn, 'norm_final': norm_final, 'w_in': w_in,
         'conv_w': conv_w, 'a_log': a_log, 'dt_bias': dt_bias, 'o_norm_w': o_norm_w,
         'sgu_norm_w': sgu_norm_w, 'w_spatial': w_spatial, 'b_spatial': b_spatial,
         'w_proj_a': w_proj_a, 'w_proj_b': w_proj_b, 'w_out': w_out, 'ffn_w_gate': ffn_w_gate,
         'ffn_w_up': ffn_w_up, 'ffn_w_down': ffn_w_down, 'router': router, 'moe_w_gate': moe_w_gate,
         'moe_w_up': moe_w_up, 'moe_w_down': moe_w_down}
    nb = x_prompt.shape[0]
    conv0 = jnp.zeros((DEPTH, nb, CONV_W - 1, QKV_DIM), x_prompt.dtype)
    delta0 = jnp.zeros((DEPTH, nb, N_HEADS, HEAD_DIM, HEAD_DIM), jnp.float32)
    y_prompt, conv_p, delta_p, _ = trunk(x_prompt, conv0, delta0, p)
    y_sample, conv_s, delta_s, chunk_v_s = trunk(x_sample, state_conv, state_delta, p)
    return (y_prompt, y_sample, delta_p, conv_p, delta_s, conv_s, chunk_v_s)
```

```python
import functools

import jax
import jax.numpy as jnp
from jax import lax
from jax.experimental import pallas as pl
from jax.experimental.pallas import tpu as pltpu

F32 = jnp.float32
BF16 = jnp.bfloat16

HEAD_DIM = 128
CHUNK = 128
DELTA_CHUNK = 64
CONV_W = 4
EPS = 1e-6

LANES = 128
V7X_VMEM_BYTES = 64 * 2**20
VMEM_LIMIT = V7X_VMEM_BYTES - 8 * 2**20


def _params(*sem):
    return pltpu.CompilerParams(dimension_semantics=sem, vmem_limit_bytes=VMEM_LIMIT)


def _tile(n, pref):
    t = pref
    while t > 8 and n % t:
        t //= 2
    assert n % t == 0, (n, pref)
    return t


def _dot(a, b):
    return jnp.dot(a.astype(BF16), b.astype(BF16), preferred_element_type=F32)


def _softplus(x):
    return jnp.maximum(x, 0.0) + jnp.log1p(jnp.exp(-jnp.abs(x)))


def _sigmoid(x):
    return jax.nn.sigmoid(x)


def _silu(x):
    return x * jax.nn.sigmoid(x)


def _gelu(x):
    return jax.nn.gelu(x)


def _ident(x):
    return x


def _rmsnorm_body(x_ref, w_ref, o_ref):
    x = x_ref[...].astype(F32)
    ms = jnp.mean(x * x, axis=-1, keepdims=True)
    o_ref[...] = (x * lax.rsqrt(ms + EPS) * w_ref[...]).astype(o_ref.dtype)


def rmsnorm(x, w, out_dtype):
    m, d = x.shape
    tm = _tile(m, 256)
    return pl.pallas_call(
        _rmsnorm_body,
        out_shape=jax.ShapeDtypeStruct((m, d), out_dtype),
        grid=(m // tm,),
        in_specs=[pl.BlockSpec((tm, d), lambda i: (i, 0)), pl.BlockSpec((1, d), lambda i: (0, 0))],
        out_specs=pl.BlockSpec((tm, d), lambda i: (i, 0)),
        compiler_params=_params("arbitrary"),
        name="rmsnorm",
    )(x, w.reshape(1, d).astype(F32))


def _proj_body(x_ref, w_ref, o_ref, *, act):
    acc = jnp.dot(x_ref[...], w_ref[...].astype(BF16), preferred_element_type=F32)
    o_ref[...] = act(acc).astype(o_ref.dtype)


def proj(x, w, layer, col0, n, act, out_dtype, tn_pref=512):
    m, k = x.shape
    tm = _tile(m, 1024)
    tn = _tile(n, tn_pref)
    assert col0 % tn == 0
    off = col0 // tn
    return pl.pallas_call(
        functools.partial(_proj_body, act=act),
        out_shape=jax.ShapeDtypeStruct((m, n), out_dtype),
        grid=(m // tm, n // tn),
        in_specs=[
            pl.BlockSpec((tm, k), lambda i, j: (i, 0)),
            pl.BlockSpec((None, k, tn), lambda i, j: (layer, 0, off + j)),
        ],
        out_specs=pl.BlockSpec((tm, tn), lambda i, j: (i, j)),
        compiler_params=_params("arbitrary", "arbitrary"),
        name="proj",
    )(x, w)


def _merge_body(ap_ref, as_ref, bp_ref, bs_ref, wa_ref, wb_ref, ga_ref, gb_ref, o_ref, *, n_p):
    is_p = pl.program_id(0) < n_p
    a = jnp.where(is_p, ap_ref[...], as_ref[...])
    b = jnp.where(is_p, bp_ref[...], bs_ref[...])
    pa = jnp.dot(a, wa_ref[...].astype(BF16), preferred_element_type=F32)
    pb = jnp.dot(b, wb_ref[...].astype(BF16), preferred_element_type=F32)
    o_ref[...] = (ga_ref[...].astype(F32) * pa + gb_ref[...].astype(F32) * pb).astype(o_ref.dtype)


def merge(a_p, a_s, b_p, b_s, wa, wb, gates, layer):
    tp, ka = a_p.shape
    ts = a_s.shape[0]
    kb = b_p.shape[1]
    d = wa.shape[2]
    tm = _tile(ts, 512)
    assert tp % tm == 0
    n_p, n_s = tp // tm, ts // tm
    tn = _tile(d, 512)
    nb = d // tn
    pmap = lambda i, j: (jnp.minimum(i, n_p - 1), 0)
    smap = lambda i, j: (jnp.maximum(i - n_p, 0), 0)
    return pl.pallas_call(
        functools.partial(_merge_body, n_p=n_p),
        out_shape=jax.ShapeDtypeStruct((tp + ts, d), BF16),
        grid=(n_p + n_s, nb),
        in_specs=[
            pl.BlockSpec((tm, ka), pmap),
            pl.BlockSpec((tm, ka), smap),
            pl.BlockSpec((tm, kb), pmap),
            pl.BlockSpec((tm, kb), smap),
            pl.BlockSpec((None, ka, tn), lambda i, j: (layer, 0, j)),
            pl.BlockSpec((None, kb, tn), lambda i, j: (layer, 0, j)),
            pl.BlockSpec((tm, tn), lambda i, j: (i, j)),
            pl.BlockSpec((tm, tn), lambda i, j: (i, nb + j)),
        ],
        out_specs=pl.BlockSpec((tm, tn), lambda i, j: (i, j)),
        compiler_params=_params("arbitrary", "arbitrary"),
        name="merge",
    )(a_p, a_s, b_p, b_s, wa, wb, gates, gates)


def _resproj_body(m_ref, w_ref, x_ref, o_ref):
    acc = jnp.dot(m_ref[...], w_ref[...].astype(BF16), preferred_element_type=F32)
    o_ref[...] = x_ref[...] + acc


def resproj(mg, w, layer, x):
    m, k = mg.shape
    d = w.shape[2]
    tm = _tile(m, 1024)
    tn = _tile(d, 512)
    return pl.pallas_call(
        _resproj_body,
        out_shape=jax.ShapeDtypeStruct((m, d), F32),
        grid=(m // tm, d // tn),
        in_specs=[
            pl.BlockSpec((tm, k), lambda i, j: (i, 0)),
            pl.BlockSpec((None, k, tn), lambda i, j: (layer, 0, j)),
            pl.BlockSpec((tm, tn), lambda i, j: (i, j)),
        ],
        out_specs=pl.BlockSpec((tm, tn), lambda i, j: (i, j)),
        compiler_params=_params("arbitrary", "arbitrary"),
        name="resproj",
    )(mg, w, x)


def _spatial_body(u_ref, gv_ref, nw_ref, w_ref, bt_ref, a_ref, *v_out, n_groups, rows):
    gv = gv_ref[...]
    ms = jnp.mean(gv * gv, axis=-1, keepdims=True)
    v = gv * lax.rsqrt(ms + EPS) * nw_ref[...]
    if v_out:
        v_out[0][...] = v
    gd = gv.shape[1] // n_groups
    rr = lax.broadcasted_iota(jnp.int32, (CHUNK, CHUNK), 0)
    cc = lax.broadcasted_iota(jnp.int32, (CHUNK, CHUNK), 1)
    tril = rr >= cc
    for g in range(n_groups):
        wm = jnp.where(tril, w_ref[g], 0.0).astype(BF16)
        bcol = bt_ref[:, g:g + 1]
        for c in range(rows // CHUNK):
            rs = slice(c * CHUNK, (c + 1) * CHUNK)
            cs = slice(g * gd, (g + 1) * gd)
            s = jnp.dot(wm, v[rs, cs].astype(BF16), preferred_element_type=F32) + bcol
            a_ref[rs, cs] = (u_ref[rs, cs].astype(F32) * s).astype(a_ref.dtype)


def spatial(u, gv, nw, w_s, b_t, row0, nrows, write_v):
    _, aw = u.shape
    n_groups = w_s.shape[0]
    rows = _tile(nrows, 256)
    assert rows % CHUNK == 0 and row0 % rows == 0
    r0 = row0 // rows
    out_shape = [jax.ShapeDtypeStruct((nrows, aw), BF16)]
    out_specs = [pl.BlockSpec((rows, aw), lambda i: (i, 0))]
    if write_v:
        out_shape.append(jax.ShapeDtypeStruct((nrows, aw), F32))
        out_specs.append(pl.BlockSpec((rows, aw), lambda i: (i, 0)))
    return pl.pallas_call(
        functools.partial(_spatial_body, n_groups=n_groups, rows=rows),
        out_shape=out_shape,
        grid=(nrows // rows,),
        in_specs=[
            pl.BlockSpec((rows, aw), lambda i: (r0 + i, 0)),
            pl.BlockSpec((rows, aw), lambda i: (r0 + i, 0)),
            pl.BlockSpec((1, aw), lambda i: (0, 0)),
            pl.BlockSpec((n_groups, CHUNK, CHUNK), lambda i: (0, 0, 0)),
            pl.BlockSpec((CHUNK, n_groups), lambda i: (0, 0)),
        ],
        out_specs=out_specs,
        compiler_params=_params("arbitrary"),
        name="spatial",
    )(u, gv, nw.reshape(1, aw), w_s, b_t)


def _conv_body(x_ref, buf_ref, cw_ref, o_ref, cs_ref, *, nb, seq, bw, tc):
    j = pl.program_id(1)
    is_q = j < bw // tc
    is_v = j >= 2 * (bw // tc)
    scale = jnp.where(is_q, HEAD_DIM ** -0.5, 1.0).astype(F32)
    hd = CONV_W - 1
    cw = cw_ref[...]
    for b in range(nb):
        cs_ref[8 - hd:8, :] = buf_ref[b]
        cs_ref[8:8 + seq, :] = x_ref[b * seq:(b + 1) * seq, :]
        acc = cs_ref[8 - hd:8 - hd + seq, :] * cw[0:1]
        for t in range(1, CONV_W):
            acc = acc + cs_ref[8 - hd + t:8 - hd + t + seq, :] * cw[t:t + 1]
        y = _silu(acc)
        parts = []
        for g in range(tc // HEAD_DIM):
            yg = y[:, g * HEAD_DIM:(g + 1) * HEAD_DIM]
            ss = jnp.sum(yg * yg, axis=-1, keepdims=True)
            parts.append(yg * lax.rsqrt(ss + EPS) * scale)
        yn = jnp.concatenate(parts, axis=1) if len(parts) > 1 else parts[0]
        o_ref[b * seq:(b + 1) * seq, :] = jnp.where(is_v, y, yn)


def conv_prep(qkv, buf, conv_w, layer, row0, n_seq, seq, nb, bw):
    _, c3 = qkv.shape
    tc = _tile(bw, 512)
    rows = nb * seq
    assert row0 % rows == 0 and n_seq % nb == 0
    r0 = row0 // rows
    return pl.pallas_call(
        functools.partial(_conv_body, nb=nb, seq=seq, bw=bw, tc=tc),
        out_shape=jax.ShapeDtypeStruct((n_seq * seq, c3), F32),
        grid=(n_seq // nb, c3 // tc),
        in_specs=[
            pl.BlockSpec((rows, tc), lambda i, j: (r0 + i, j)),
            pl.BlockSpec((None, nb, CONV_W - 1, tc), lambda i, j: (layer, i, 0, j)),
            pl.BlockSpec((None, CONV_W, tc), lambda i, j: (layer, 0, j)),
        ],
        out_specs=pl.BlockSpec((rows, tc), lambda i, j: (i, j)),
        scratch_shapes=[pltpu.VMEM((8 + seq, tc), F32)],
        compiler_params=_params("arbitrary", "arbitrary"),
        name="conv_prep",
    )(qkv, buf, conv_w)


def _inv_unit_lower(a, rr, cc):
    eye = (rr == cc).astype(F32)
    b16 = (rr >> 4) == (cc >> 4)
    b32 = (rr >> 5) == (cc >> 5)
    n1 = jnp.where(b16, a, 0.0)
    m1 = jnp.where(b32 & jnp.logical_not(b16), a, 0.0)
    m2 = jnp.where(b32, 0.0, a)
    n2 = _dot(n1, n1)
    p = eye - n1
    p = p + _dot(p, n2)
    n4 = _dot(n2, n2)
    p = p + _dot(p, n4)
    n8 = _dot(n4, n4)
    p = p + _dot(p, n8)
    p = p - _dot(p, _dot(m1, p))
    p = p - _dot(p, _dot(m2, p))
    return p


def _delta_body(act_ref, z_ref, bd_ref, s0_ref, alog_ref, dtb_ref, onw_ref, o_ref, s_ref, *, n_heads, rows, nsq):
    c = DELTA_CHUNK
    hd = HEAD_DIM
    bw = n_heads * hd
    pad = c - rows

    @pl.when(pl.program_id(1) == 0)
    def _():
        s_ref[...] = s0_ref[...]

    def padrows(x):
        if pad == 0:
            return x
        return jnp.concatenate([x, jnp.zeros((pad, x.shape[1]), x.dtype)], axis=0)

    rowi = lax.broadcasted_iota(jnp.int32, (c, LANES), 0)
    rr = lax.broadcasted_iota(jnp.int32, (2 * c, 2 * c), 0)
    cc = lax.broadcasted_iota(jnp.int32, (2 * c, 2 * c), 1)
    left = cc < c
    same = (rr < c) == left
    incl = same & (rr >= cc)
    strict = same & (rr > cc)
    zpad = jnp.zeros((c, hd), F32)
    onw = onw_ref[...]
    zf = z_ref[...].astype(F32)
    alog = alog_ref[...]
    dtb = dtb_ref[...]

    def colpair(m, l0, l1):
        return jnp.concatenate([m[:, l0:l0 + 1], m[:, l1:l1 + 1]], axis=0)

    outs = [[None] * n_heads for _ in range(nsq)]
    for sq in range(nsq):
        rs = slice(sq * rows, (sq + 1) * rows)
        bd = bd_ref[rs, :]
        beta_all = padrows(_sigmoid(bd))
        gc = padrows(-jnp.exp(alog) * _softplus(bd + dtb))
        sh = 1
        while sh < c:
            gc = gc + jnp.where(rowi >= sh, pltpu.roll(gc, sh, axis=0), 0.0)
            sh *= 2
        egc = jnp.exp(gc)
        gl = gc[c - 1:c, :]
        ekd = jnp.exp(gl - gc)
        egl = jnp.exp(gl)

        def headpair(base, h0, h1):
            return jnp.concatenate([padrows(act_ref[rs, base + h0 * hd:base + (h0 + 1) * hd]),
                                    padrows(act_ref[rs, base + h1 * hd:base + (h1 + 1) * hd])], axis=0)

        for p in range(n_heads // 2):
            h0, h1 = 2 * p, 2 * p + 1
            q = headpair(0, h0, h1)
            k = headpair(bw, h0, h1)
            v = headpair(2 * bw, h0, h1)
            b_col = colpair(beta_all, h0, h1)
            gc_col = colpair(gc, n_heads + h0, n_heads + h1)
            egc_col = colpair(egc, n_heads + h0, n_heads + h1)
            ekd_col = colpair(ekd, n_heads + h0, n_heads + h1)
            g_i = jnp.broadcast_to(gc_col, (2 * c, 2 * c))
            g_j = g_i.T
            dec = jnp.where(incl, jnp.exp(jnp.where(incl, g_i - g_j, 0.0)), 0.0)
            kb = k * b_col
            x = lax.dot_general(jnp.concatenate([kb, q], axis=0).astype(BF16), k.astype(BF16),
                                (((1,), (1,)), ((), ())), preferred_element_type=F32)
            a = jnp.where(strict, x[:2 * c], 0.0) * dec
            qk = x[2 * c:] * dec
            t = _inv_unit_lower(a, rr, cc)
            r = _dot(t, jnp.concatenate([v * b_col, kb * egc_col], axis=1))
            u = r[:, :hd]
            w = r[:, hd:]
            qd = q * egc_col
            s_h0 = s_ref[sq, h0]
            s_h1 = s_ref[sq, h1]
            lhs_y = jnp.concatenate([
                jnp.concatenate([w[:c], zpad], axis=1),
                jnp.concatenate([zpad, w[c:]], axis=1),
                jnp.concatenate([qd[:c], zpad], axis=1),
                jnp.concatenate([zpad, qd[c:]], axis=1),
            ], axis=0)
            y = _dot(lhs_y, jnp.concatenate([s_h0, s_h1], axis=0))
            v_new = u - y[:2 * c]
            kdt = (k * ekd_col).T
            lhs_z = jnp.concatenate([qk, jnp.where(left, kdt, 0.0), jnp.where(left, 0.0, kdt)], axis=0)
            zz = _dot(lhs_z, v_new)
            o = y[2 * c:] + zz[:2 * c]
            s_ref[sq, h0] = s_h0 * egl[:, n_heads + h0:n_heads + h0 + 1] + zz[2 * c:2 * c + hd]
            s_ref[sq, h1] = s_h1 * egl[:, n_heads + h1:n_heads + h1 + 1] + zz[2 * c + hd:]
            ms = jnp.mean(o * o, axis=-1, keepdims=True)
            on = o * lax.rsqrt(ms + EPS) * onw
            outs[sq][h0] = on[:rows] * zf[rs, h0 * hd:(h0 + 1) * hd]
            outs[sq][h1] = on[c:c + rows] * zf[rs, h1 * hd:(h1 + 1) * hd]

    for h in range(n_heads):
        blk = outs[0][h] if nsq == 1 else jnp.concatenate([outs[sq][h] for sq in range(nsq)], axis=0)
        o_ref[:, h * hd:(h + 1) * hd] = blk.astype(o_ref.dtype)


def delta_rule(act, zs, bd, s0, alog_row, dtb_row, onw, layer, row0, n_seq, seq):
    _, c3 = act.shape
    bw = c3 // 3
    n_heads = bw // HEAD_DIM
    assert n_heads % 2 == 0 and 2 * n_heads <= LANES
    rows = min(seq, DELTA_CHUNK)
    n_chunks = seq // rows
    nsq = 1 if rows >= 16 else 16 // rows
    brows = nsq * rows
    assert seq % rows == 0 and rows % 8 == 0 and row0 % brows == 0 and n_seq % nsq == 0
    assert nsq == 1 or n_chunks == 1
    r0 = row0 // brows
    hd = HEAD_DIM
    if s0.ndim == 5:
        s0_spec = pl.BlockSpec((None, nsq, n_heads, hd, hd), lambda b, c: (layer, b, 0, 0, 0))
    else:
        s0_spec = pl.BlockSpec((nsq, n_heads, hd, hd), lambda b, c: (b, 0, 0, 0))
    rowmap = lambda b, c: (b * n_chunks + c, 0)
    rowmap_t = lambda b, c: (r0 + b * n_chunks + c, 0)
    return pl.pallas_call(
        functools.partial(_delta_body, n_heads=n_heads, rows=rows, nsq=nsq),
        out_shape=[jax.ShapeDtypeStruct((n_seq * seq, bw), BF16),
                   jax.ShapeDtypeStruct((n_seq, n_heads, hd, hd), F32)],
        grid=(n_seq // nsq, n_chunks),
        in_specs=[
            pl.BlockSpec((brows, c3), rowmap),
            pl.BlockSpec((brows, bw), rowmap_t),
            pl.BlockSpec((brows, LANES), rowmap_t),
            s0_spec,
            pl.BlockSpec((1, LANES), lambda b, c: (0, 0)),
            pl.BlockSpec((1, LANES), lambda b, c: (0, 0)),
            pl.BlockSpec((1, hd), lambda b, c: (0, 0)),
        ],
        out_specs=[pl.BlockSpec((brows, bw), rowmap),
                   pl.BlockSpec((nsq, n_heads, hd, hd), lambda b, c: (b, 0, 0, 0))],
        compiler_params=_params("arbitrary", "arbitrary"),
        name="delta_rule",
    )(act, zs, bd, s0, alog_row, dtb_row, onw)


def _gate_up_body(te_ref, tb_ref, tf_ref, na_ref, x_ref, wg_ref, wu_ref, o_ref, wgb, wub):
    i = pl.program_id(1)

    @pl.when(tf_ref[i] == 1)
    def _():
        wgb[...] = wg_ref[...].astype(BF16)
        wub[...] = wu_ref[...].astype(BF16)

    @pl.when(i < na_ref[0])
    def _():
        x = x_ref[...]
        a = jnp.dot(x, wgb[...], preferred_element_type=F32)
        b = jnp.dot(x, wub[...], preferred_element_type=F32)
        o_ref[...] = (_silu(a) * b).astype(o_ref.dtype)

    @pl.when(i >= na_ref[0])
    def _():
        o_ref[...] = jnp.zeros_like(o_ref)


def gate_up(xs, wg, wu, widx, tile_expert, tile_blk, tile_first, n_active, tm):
    p, k = xs.shape
    f = wg.shape[-1]
    tn = _tile(f, 512)
    n_tiles = p // tm
    grid_spec = pltpu.PrefetchScalarGridSpec(
        num_scalar_prefetch=4,
        grid=(f // tn, n_tiles),
        in_specs=[
            pl.BlockSpec((tm, k), lambda j, i, te, tb, tf, na: (tb[i], 0)),
            pl.BlockSpec((None, None, k, tn), lambda j, i, te, tb, tf, na: (widx, te[i], 0, j)),
            pl.BlockSpec((None, None, k, tn), lambda j, i, te, tb, tf, na: (widx, te[i], 0, j)),
        ],
        out_specs=pl.BlockSpec((tm, tn), lambda j, i, te, tb, tf, na: (i, j)),
        scratch_shapes=[pltpu.VMEM((k, tn), BF16), pltpu.VMEM((k, tn), BF16)],
    )
    return pl.pallas_call(
        _gate_up_body,
        out_shape=jax.ShapeDtypeStruct((p, f), BF16),
        grid_spec=grid_spec,
        compiler_params=_params("arbitrary", "arbitrary"),
        name="ffn_gate_up",
    )(tile_expert, tile_blk, tile_first, n_active, xs, wg, wu)


def _down_body(te_ref, tb_ref, na_ref, h_ref, w_ref, *rest, nk, residual):
    if residual:
        x_ref, o_ref, acc = rest
    else:
        o_ref, acc = rest
    i = pl.program_id(1)
    kk = pl.program_id(2)

    @pl.when(i < na_ref[0])
    def _():
        @pl.when(kk == 0)
        def _():
            acc[...] = jnp.zeros_like(acc)

        acc[...] += jnp.dot(h_ref[...], w_ref[...].astype(BF16), preferred_element_type=F32)

        @pl.when(kk == nk - 1)
        def _():
            if residual:
                o_ref[...] = x_ref[...] + acc[...]
            else:
                o_ref[...] = acc[...]

    @pl.when((i >= na_ref[0]) & (kk == 0))
    def _():
        o_ref[...] = jnp.zeros_like(o_ref)


def down(h, wd, widx, tile_expert, tile_blk, n_active, tm, x_res=None):
    p, f = h.shape
    d = wd.shape[-1]
    tn = _tile(d, 1024)
    tk = _tile(f, 512)
    nk = f // tk
    n_tiles = p // tm

    def kmap(i, kk, na):
        return jnp.where(i < na[0], kk, nk - 1)

    in_specs = [
        pl.BlockSpec((tm, tk), lambda n, i, kk, te, tb, na: (tb[i], kmap(i, kk, na))),
        pl.BlockSpec((None, None, tk, tn), lambda n, i, kk, te, tb, na: (widx, te[i], kmap(i, kk, na), n)),
    ]
    args = [tile_expert, tile_blk, n_active, h, wd]
    if x_res is not None:
        in_specs.append(pl.BlockSpec((tm, tn), lambda n, i, kk, te, tb, na: (tb[i], n)))
        args.append(x_res)
    grid_spec = pltpu.PrefetchScalarGridSpec(
        num_scalar_prefetch=3,
        grid=(d // tn, n_tiles, nk),
        in_specs=in_specs,
        out_specs=pl.BlockSpec((tm, tn), lambda n, i, kk, te, tb, na: (i, n)),
        scratch_shapes=[pltpu.VMEM((tm, tn), F32)],
    )
    return pl.pallas_call(
        functools.partial(_down_body, nk=nk, residual=x_res is not None),
        out_shape=jax.ShapeDtypeStruct((p, d), F32),
        grid_spec=grid_spec,
        compiler_params=_params("arbitrary", "arbitrary", "arbitrary"),
        name="ffn_down",
    )(*args)


def _router_body(x_ref, nw_ref, r_ref, xn_ref, comb_ref, sel_ref, *, n_experts):
    x = x_ref[...]
    ms = jnp.mean(x * x, axis=-1, keepdims=True)
    xn = x * lax.rsqrt(ms + EPS) * nw_ref[...]
    xn_ref[...] = xn
    logits = jnp.dot(xn, r_ref[...], preferred_element_type=F32, precision=lax.Precision.HIGHEST)
    lane = lax.broadcasted_iota(jnp.int32, logits.shape, 1)
    neg = jnp.float32(-jnp.inf)
    lg = jnp.where(lane < n_experts, logits, neg)
    m1 = jnp.max(lg, axis=-1, keepdims=True)
    i1 = jnp.min(jnp.where(lg == m1, lane, LANES), axis=-1, keepdims=True)
    lg2 = jnp.where(lane == i1, neg, lg)
    m2 = jnp.max(lg2, axis=-1, keepdims=True)
    i2 = jnp.min(jnp.where(lg2 == m2, lane, LANES), axis=-1, keepdims=True)
    e2 = jnp.exp(m2 - m1)
    den = 1.0 + e2
    comb_ref[...] = jnp.where(lane == i1, 1.0 / den, 0.0) + jnp.where(lane == i2, e2 / den, 0.0)
    sel_ref[...] = ((lane == i1) | (lane == i2)).astype(jnp.int32)


def router_topk(x, nw, router_pad, n_experts):
    m, d = x.shape
    tm = _tile(m, 256)
    return pl.pallas_call(
        functools.partial(_router_body, n_experts=n_experts),
        out_shape=[jax.ShapeDtypeStruct((m, d), F32),
                   jax.ShapeDtypeStruct((m, LANES), F32),
                   jax.ShapeDtypeStruct((m, LANES), jnp.int32)],
        grid=(m // tm,),
        in_specs=[pl.BlockSpec((tm, d), lambda i: (i, 0)),
                  pl.BlockSpec((1, d), lambda i: (0, 0)),
                  pl.BlockSpec((d, LANES), lambda i: (0, 0))],
        out_specs=[pl.BlockSpec((tm, d), lambda i: (i, 0)),
                   pl.BlockSpec((tm, LANES), lambda i: (i, 0)),
                   pl.BlockSpec((tm, LANES), lambda i: (i, 0))],
        compiler_params=_params("arbitrary"),
        name="router_topk",
    )(x, nw.reshape(1, d), router_pad)


def _row_copy(src_hbm, row, dst, r, sem):
    return pltpu.make_async_copy(src_hbm.at[pl.ds(row, 1), :], dst.at[pl.ds(r, 1), :], sem)


def _gather_body(idx_ref, x_hbm, o_ref, buf, sem, *, rows):
    base = pl.program_id(0) * rows

    def start(r, carry):
        _row_copy(x_hbm, idx_ref[base + r], buf, r, sem).start()
        return carry

    lax.fori_loop(0, rows, start, 0)

    def wait(r, carry):
        _row_copy(x_hbm, 0, buf, r, sem).wait()
        return carry

    lax.fori_loop(0, rows, wait, 0)
    o_ref[...] = buf[...].astype(o_ref.dtype)


def gather_rows(x, src_rows, out_dtype):
    _, d = x.shape
    p = src_rows.shape[0]
    rows = _tile(p, 256)
    grid_spec = pltpu.PrefetchScalarGridSpec(
        num_scalar_prefetch=1,
        grid=(p // rows,),
        in_specs=[pl.BlockSpec(memory_space=pl.ANY)],
        out_specs=pl.BlockSpec((rows, d), lambda i, idx: (i, 0)),
        scratch_shapes=[pltpu.VMEM((rows, d), x.dtype), pltpu.SemaphoreType.DMA],
    )
    return pl.pallas_call(
        functools.partial(_gather_body, rows=rows),
        out_shape=jax.ShapeDtypeStruct((p, d), out_dtype),
        grid_spec=grid_spec,
        compiler_params=_params("arbitrary"),
        name="moe_gather",
    )(src_rows, x)


def _combine_body(p0_ref, p1_ref, x_ref, g0_ref, g1_ref, o_hbm, y_ref, buf_a, buf_b, sems, *, rows):
    base = pl.program_id(0) * rows

    def start(r, carry):
        _row_copy(o_hbm, p0_ref[base + r], buf_a, r, sems.at[0]).start()
        _row_copy(o_hbm, p1_ref[base + r], buf_b, r, sems.at[1]).start()
        return carry

    lax.fori_loop(0, rows, start, 0)

    def wait(r, carry):
        _row_copy(o_hbm, 0, buf_a, r, sems.at[0]).wait()
        _row_copy(o_hbm, 0, buf_b, r, sems.at[1]).wait()
        return carry

    lax.fori_loop(0, rows, wait, 0)
    g0 = g0_ref[...]
    g1 = g1_ref[...]
    for cblk in range(x_ref.shape[1] // LANES):
        cs = slice(cblk * LANES, (cblk + 1) * LANES)
        y_ref[:, cs] = x_ref[:, cs] + g0 * buf_a[:, cs] + g1 * buf_b[:, cs]


def combine(x, o_rows, pos0, pos1, g0b, g1b):
    m, d = x.shape
    rows = _tile(m, 256)
    grid_spec = pltpu.PrefetchScalarGridSpec(
        num_scalar_prefetch=2,
        grid=(m // rows,),
        in_specs=[pl.BlockSpec((rows, d), lambda i, a, b: (i, 0)),
                  pl.BlockSpec((rows, LANES), lambda i, a, b: (i, 0)),
                  pl.BlockSpec((rows, LANES), lambda i, a, b: (i, 0)),
                  pl.BlockSpec(memory_space=pl.ANY)],
        out_specs=pl.BlockSpec((rows, d), lambda i, a, b: (i, 0)),
        scratch_shapes=[pltpu.VMEM((rows, d), F32), pltpu.VMEM((rows, d), F32),
                        pltpu.SemaphoreType.DMA((2,))],
    )
    return pl.pallas_call(
        functools.partial(_combine_body, rows=rows),
        out_shape=jax.ShapeDtypeStruct((m, d), F32),
        grid_spec=grid_spec,
        compiler_params=_params("arbitrary"),
        name="moe_combine",
    )(pos0, pos1, x, g0b, g1b, o_rows)


def _dispatch_plan(sel, comb, n_experts, tm):
    t = sel.shape[0]
    n_tiles = (TOP_K_ROWS(t) + n_experts * tm) // tm
    sel_e = sel[:, :n_experts]
    csum = jnp.cumsum(sel_e, axis=0)
    counts = csum[-1]
    rank = csum - sel_e
    ptiles = (counts + tm - 1) // tm
    tile_end = jnp.cumsum(ptiles)
    row_start = (tile_end - ptiles) * tm
    pos = row_start[None, :] + rank
    n_active = tile_end[-1]
    tiles = jnp.arange(n_tiles, dtype=jnp.int32)
    tile_blk = jnp.minimum(tiles, n_active - 1).astype(jnp.int32)
    tile_expert = jnp.minimum(jnp.searchsorted(tile_end, tile_blk, side="right"), n_experts - 1).astype(jnp.int32)
    prev = jnp.concatenate([jnp.full((1,), -1, jnp.int32), tile_expert[:-1]])
    tile_first = ((tile_expert != prev) & (tiles < n_active)).astype(jnp.int32)
    p_rows = n_tiles * tm
    tok = jnp.broadcast_to(jnp.arange(t, dtype=jnp.int32)[:, None], pos.shape)
    flat_pos = jnp.where(sel_e > 0, pos, p_rows).reshape(-1)
    src_rows = jnp.zeros((p_rows,), jnp.int32).at[flat_pos].set(tok.reshape(-1), mode="drop")
    big = jnp.int32(2**30)
    pos_lo = jnp.min(jnp.where(sel_e > 0, pos, big), axis=1).astype(jnp.int32)
    pos_hi = jnp.max(jnp.where(sel_e > 0, pos, -1), axis=1).astype(jnp.int32)
    e_lo = jnp.argmin(jnp.where(sel_e > 0, pos, big), axis=1)
    e_hi = jnp.argmax(jnp.where(sel_e > 0, pos, -1), axis=1)
    comb_e = comb[:, :n_experts]
    g_lo = jnp.take_along_axis(comb_e, e_lo[:, None], axis=1)
    g_hi = jnp.take_along_axis(comb_e, e_hi[:, None], axis=1)
    g0b = jnp.broadcast_to(g_lo, (t, LANES))
    g1b = jnp.broadcast_to(g_hi, (t, LANES))
    return (tile_expert, tile_blk, tile_first, n_active.reshape(1).astype(jnp.int32), src_rows,
            pos_lo, pos_hi, g0b, g1b)


def TOP_K_ROWS(t):
    return 2 * t


def _dense_plan(t, tm):
    n_tiles = t // tm
    z = jnp.zeros((n_tiles,), jnp.int32)
    first = z.at[0].set(1)
    return z, jnp.arange(n_tiles, dtype=jnp.int32), first, jnp.full((1,), n_tiles, jnp.int32)


def kernel(x_prompt, x_sample, state_delta, state_conv, norm_mix, norm_ffn, norm_final, w_in, conv_w, a_log, dt_bias, o_norm_w, sgu_norm_w, w_spatial, b_spatial, w_proj_a, w_proj_b, w_out, ffn_w_gate, ffn_w_up, ffn_w_down, router, moe_w_gate, moe_w_up, moe_w_down):
    bp, lp, d = x_prompt.shape
    bs, ls, _ = x_sample.shape
    depth = norm_mix.shape[0]
    n_groups = w_spatial.shape[1]
    aw = sgu_norm_w.shape[1]
    n_heads = a_log.shape[1]
    bw = n_heads * HEAD_DIM
    qkv_dim = 3 * bw
    n_experts = router.shape[2]
    tp, ts = bp * lp, bs * ls
    t = tp + ts
    c2 = 2 * aw
    c3 = c2 + qkv_dim
    c4 = c3 + bw
    c6 = c4 + 2 * n_heads
    assert lp % CHUNK == 0 and ls <= CHUNK and CHUNK % ls == 0 and ls >= CONV_W - 1
    assert c4 % LANES == 0 and w_in.shape[2] == c6 + 2 * d

    x = jnp.concatenate([x_prompt.reshape(tp, d), x_sample.reshape(ts, d)], axis=0)
    w_gate = w_in[:, :, c6:]
    zero_conv = jnp.zeros((depth, bp, CONV_W - 1, qkv_dim), F32)
    zero_delta = jnp.zeros((bp, n_heads, HEAD_DIM, HEAD_DIM), F32)
    rep = CHUNK // ls
    eye_rep = jnp.eye(rep, dtype=F32)
    lane_pad = LANES - 2 * n_heads

    tm_ffn = _tile(t, 512)
    new_conv_p, new_conv_s, new_delta_p, new_delta_s, chunk_v = [], [], [], [], []
    for l in range(depth):
        xn = rmsnorm(x, norm_mix[l], BF16)
        u = proj(xn, w_in, l, 0, aw, _gelu, BF16)
        gv = proj(xn, w_in, l, aw, aw, _gelu, F32)
        qkv = proj(xn, w_in, l, c2, qkv_dim, _ident, F32)
        zs = proj(xn, w_in, l, c3, bw, _silu, BF16)
        bd = proj(xn, w_in, l, c4, LANES, _ident, F32, tn_pref=LANES)
        gates = proj(xn, w_gate, l, 0, 2 * d, _sigmoid, BF16)

        w_s = w_spatial[l]
        b_t = b_spatial[l].T
        w_s_small = jnp.einsum("ab,gts->gatbs", eye_rep, w_s[:, :ls, :ls]).reshape(n_groups, CHUNK, CHUNK)
        b_t_small = jnp.tile(b_spatial[l][:, :ls], (1, rep)).T
        (a_p,) = spatial(u, gv, sgu_norm_w[l], w_s, b_t, 0, tp, False)
        a_s, v_s = spatial(u, gv, sgu_norm_w[l], w_s_small, b_t_small, tp, ts, True)
        chunk_v.append(v_s.reshape(bs, ls, aw))

        nb_s = _tile(bs, max(1, 128 // ls))
        act_p = conv_prep(qkv, zero_conv, conv_w, l, 0, bp, lp, 1, bw)
        act_s = conv_prep(qkv, state_conv, conv_w, l, tp, bs, ls, nb_s, bw)
        alog_row = jnp.pad(a_log[l], (n_heads, lane_pad)).reshape(1, LANES)
        dtb_row = jnp.pad(dt_bias[l], (n_heads, lane_pad)).reshape(1, LANES)
        onw = o_norm_w[l].reshape(1, HEAD_DIM)
        b_p, s_p = delta_rule(act_p, zs, bd, zero_delta, alog_row, dtb_row, onw, l, 0, bp, lp)
        b_s, s_s = delta_rule(act_s, zs, bd, state_delta, alog_row, dtb_row, onw, l, tp, bs, ls)
        new_delta_p.append(s_p)
        new_delta_s.append(s_s)
        new_conv_p.append(qkv[:tp].reshape(bp, lp, qkv_dim)[:, lp - (CONV_W - 1):])
        new_conv_s.append(qkv[tp:].reshape(bs, ls, qkv_dim)[:, ls - (CONV_W - 1):])

        mg = merge(a_p, a_s, b_p, b_s, w_proj_a, w_proj_b, gates, l)
        x = resproj(mg, w_out, l, x)

        j = l // 2
        if l % 2 == 0:
            xf = rmsnorm(x, norm_ffn[l], BF16)
            te, tb, tf, na = _dense_plan(t, tm_ffn)
            h = gate_up(xf, ffn_w_gate[:, None], ffn_w_up[:, None], j, te, tb, tf, na, tm_ffn)
            x = down(h, ffn_w_down[:, None], j, te, tb, na, tm_ffn, x_res=x)
        else:
            router_pad = jnp.pad(router[j], ((0, 0), (0, LANES - n_experts)))
            xf32, comb, sel = router_topk(x, norm_ffn[l], router_pad, n_experts)
            te, tb, tf, na, src_rows, p0, p1, g0b, g1b = _dispatch_plan(sel, comb, n_experts, tm_ffn)
            xs = gather_rows(xf32, src_rows, BF16)
            h = gate_up(xs, moe_w_gate, moe_w_up, j, te, tb, tf, na, tm_ffn)
            o_rows = down(h, moe_w_down, j, te, tb, na, tm_ffn)
            x = combine(x, o_rows, p0, p1, g0b, g1b)

    y = rmsnorm(x, norm_final, F32)
    y_prompt = y[:tp].reshape(bp, lp, d)
    y_sample = y[tp:].reshape(bs, ls, d)
    return (y_prompt, y_sample, jnp.stack(new_delta_p), jnp.stack(new_conv_p),
            jnp.stack(new_delta_s), jnp.stack(new_conv_s), jnp.stack(chunk_v))
```

```python
import functools

import jax
import jax.numpy as jnp
from jax import lax
from jax.experimental import pallas as pl
from jax.experimental.pallas import tpu as pltpu

F32 = jnp.float32
BF16 = jnp.bfloat16

HEAD_DIM = 128
CHUNK = 128
DELTA_CHUNK = 64
CONV_W = 4
EPS = 1e-6

LANES = 128
V7X_VMEM_BYTES = 64 * 2**20
VMEM_LIMIT = V7X_VMEM_BYTES - 8 * 2**20


def _params(*sem):
    return pltpu.CompilerParams(dimension_semantics=sem, vmem_limit_bytes=VMEM_LIMIT)


def _tile(n, pref):
    t = pref
    while t > 8 and n % t:
        t //= 2
    assert n % t == 0, (n, pref)
    return t


def _dot(a, b):
    return jnp.dot(a.astype(BF16), b.astype(BF16), preferred_element_type=F32)


def _softplus(x):
    return jnp.maximum(x, 0.0) + jnp.log1p(jnp.exp(-jnp.abs(x)))


def _sigmoid(x):
    return jax.nn.sigmoid(x)


def _silu(x):
    return x * jax.nn.sigmoid(x)


def _gelu(x):
    return jax.nn.gelu(x)


def _ident(x):
    return x


def _rmsnorm_body(x_ref, w_ref, o_ref):
    x = x_ref[...].astype(F32)
    ms = jnp.mean(x * x, axis=-1, keepdims=True)
    o_ref[...] = (x * lax.rsqrt(ms + EPS) * w_ref[...]).astype(o_ref.dtype)


def rmsnorm(x, w, out_dtype):
    m, d = x.shape
    tm = _tile(m, 256)
    return pl.pallas_call(
        _rmsnorm_body,
        out_shape=jax.ShapeDtypeStruct((m, d), out_dtype),
        grid=(m // tm,),
        in_specs=[pl.BlockSpec((tm, d), lambda i: (i, 0)), pl.BlockSpec((1, d), lambda i: (0, 0))],
        out_specs=pl.BlockSpec((tm, d), lambda i: (i, 0)),
        compiler_params=_params("arbitrary"),
        name="rmsnorm",
    )(x, w.reshape(1, d).astype(F32))


def _rmsnorm_split_body(x_ref, w_ref, op_ref, os_ref, *, n_p):
    x = x_ref[...]
    ms = jnp.mean(x * x, axis=-1, keepdims=True)
    y = x * lax.rsqrt(ms + EPS) * w_ref[...]
    i = pl.program_id(0)

    @pl.when(i < n_p)
    def _():
        op_ref[...] = y

    @pl.when(i >= n_p)
    def _():
        os_ref[...] = y


def rmsnorm_split(x, w, tp):
    m, d = x.shape
    ts = m - tp
    tm = _tile(ts, 256)
    assert tp % tm == 0
    n_p = tp // tm
    return pl.pallas_call(
        functools.partial(_rmsnorm_split_body, n_p=n_p),
        out_shape=[jax.ShapeDtypeStruct((tp, d), F32), jax.ShapeDtypeStruct((ts, d), F32)],
        grid=(m // tm,),
        in_specs=[pl.BlockSpec((tm, d), lambda i: (i, 0)), pl.BlockSpec((1, d), lambda i: (0, 0))],
        out_specs=[pl.BlockSpec((tm, d), lambda i: (jnp.minimum(i, n_p - 1), 0)),
                   pl.BlockSpec((tm, d), lambda i: (jnp.maximum(i - n_p, 0), 0))],
        compiler_params=_params("arbitrary"),
        name="rmsnorm_split",
    )(x, w.reshape(1, d).astype(F32))


def _proj_body(x_ref, w_ref, o_ref, *, act):
    acc = jnp.dot(x_ref[...], w_ref[...].astype(BF16), preferred_element_type=F32)
    o_ref[...] = act(acc).astype(o_ref.dtype)


def proj(x, w, layer, col0, n, act, out_dtype, tn_pref=512):
    m, k = x.shape
    tm = _tile(m, 1024)
    tn = _tile(n, tn_pref)
    assert col0 % tn == 0
    off = col0 // tn
    return pl.pallas_call(
        functools.partial(_proj_body, act=act),
        out_shape=jax.ShapeDtypeStruct((m, n), out_dtype),
        grid=(m // tm, n // tn),
        in_specs=[
            pl.BlockSpec((tm, k), lambda i, j: (i, 0)),
            pl.BlockSpec((None, k, tn), lambda i, j: (layer, 0, off + j)),
        ],
        out_specs=pl.BlockSpec((tm, tn), lambda i, j: (i, j)),
        compiler_params=_params("arbitrary", "arbitrary"),
        name="proj",
    )(x, w)


def _merge_body(ap_ref, as_ref, bp_ref, bs_ref, wa_ref, wb_ref, ga_ref, gb_ref, o_ref, *, n_p):
    is_p = pl.program_id(0) < n_p
    a = jnp.where(is_p, ap_ref[...], as_ref[...])
    b = jnp.where(is_p, bp_ref[...], bs_ref[...])
    pa = jnp.dot(a, wa_ref[...].astype(BF16), preferred_element_type=F32)
    pb = jnp.dot(b, wb_ref[...].astype(BF16), preferred_element_type=F32)
    o_ref[...] = (ga_ref[...].astype(F32) * pa + gb_ref[...].astype(F32) * pb).astype(o_ref.dtype)


def merge(a_p, a_s, b_p, b_s, wa, wb, gates, layer):
    tp, ka = a_p.shape
    ts = a_s.shape[0]
    kb = b_p.shape[1]
    d = wa.shape[2]
    tm = _tile(ts, 512)
    assert tp % tm == 0
    n_p, n_s = tp // tm, ts // tm
    tn = _tile(d, 512)
    nb = d // tn
    pmap = lambda i, j: (jnp.minimum(i, n_p - 1), 0)
    smap = lambda i, j: (jnp.maximum(i - n_p, 0), 0)
    return pl.pallas_call(
        functools.partial(_merge_body, n_p=n_p),
        out_shape=jax.ShapeDtypeStruct((tp + ts, d), BF16),
        grid=(n_p + n_s, nb),
        in_specs=[
            pl.BlockSpec((tm, ka), pmap),
            pl.BlockSpec((tm, ka), smap),
            pl.BlockSpec((tm, kb), pmap),
            pl.BlockSpec((tm, kb), smap),
            pl.BlockSpec((None, ka, tn), lambda i, j: (layer, 0, j)),
            pl.BlockSpec((None, kb, tn), lambda i, j: (layer, 0, j)),
            pl.BlockSpec((tm, tn), lambda i, j: (i, j)),
            pl.BlockSpec((tm, tn), lambda i, j: (i, nb + j)),
        ],
        out_specs=pl.BlockSpec((tm, tn), lambda i, j: (i, j)),
        compiler_params=_params("arbitrary", "arbitrary"),
        name="merge",
    )(a_p, a_s, b_p, b_s, wa, wb, gates, gates)


def _resproj_body(m_ref, w_ref, x_ref, o_ref):
    acc = jnp.dot(m_ref[...], w_ref[...].astype(BF16), preferred_element_type=F32)
    o_ref[...] = x_ref[...] + acc


def resproj(mg, w, layer, x):
    m, k = mg.shape
    d = w.shape[2]
    tm = _tile(m, 1024)
    tn = _tile(d, 512)
    return pl.pallas_call(
        _resproj_body,
        out_shape=jax.ShapeDtypeStruct((m, d), F32),
        grid=(m // tm, d // tn),
        in_specs=[
            pl.BlockSpec((tm, k), lambda i, j: (i, 0)),
            pl.BlockSpec((None, k, tn), lambda i, j: (layer, 0, j)),
            pl.BlockSpec((tm, tn), lambda i, j: (i, j)),
        ],
        out_specs=pl.BlockSpec((tm, tn), lambda i, j: (i, j)),
        compiler_params=_params("arbitrary", "arbitrary"),
        name="resproj",
    )(mg, w, x)


def _spatial_body(u_ref, gv_ref, nw_ref, w_ref, bt_ref, a_ref, *v_out, n_groups, rows):
    gv = gv_ref[...]
    ms = jnp.mean(gv * gv, axis=-1, keepdims=True)
    v = gv * lax.rsqrt(ms + EPS) * nw_ref[...]
    if v_out:
        v_out[0][...] = v
    gd = gv.shape[1] // n_groups
    rr = lax.broadcasted_iota(jnp.int32, (CHUNK, CHUNK), 0)
    cc = lax.broadcasted_iota(jnp.int32, (CHUNK, CHUNK), 1)
    tril = rr >= cc
    for g in range(n_groups):
        wm = jnp.where(tril, w_ref[g], 0.0).astype(BF16)
        bcol = bt_ref[:, g:g + 1]
        for c in range(rows // CHUNK):
            rs = slice(c * CHUNK, (c + 1) * CHUNK)
            cs = slice(g * gd, (g + 1) * gd)
            s = jnp.dot(wm, v[rs, cs].astype(BF16), preferred_element_type=F32) + bcol
            a_ref[rs, cs] = (u_ref[rs, cs].astype(F32) * s).astype(a_ref.dtype)


def spatial(u, gv, nw, w_s, b_t, row0, nrows, write_v):
    _, aw = u.shape
    n_groups = w_s.shape[0]
    rows = _tile(nrows, 256)
    assert rows % CHUNK == 0 and row0 % rows == 0
    r0 = row0 // rows
    out_shape = [jax.ShapeDtypeStruct((nrows, aw), BF16)]
    out_specs = [pl.BlockSpec((rows, aw), lambda i: (i, 0))]
    if write_v:
        out_shape.append(jax.ShapeDtypeStruct((nrows, aw), F32))
        out_specs.append(pl.BlockSpec((rows, aw), lambda i: (i, 0)))
    return pl.pallas_call(
        functools.partial(_spatial_body, n_groups=n_groups, rows=rows),
        out_shape=out_shape,
        grid=(nrows // rows,),
        in_specs=[
            pl.BlockSpec((rows, aw), lambda i: (r0 + i, 0)),
            pl.BlockSpec((rows, aw), lambda i: (r0 + i, 0)),
            pl.BlockSpec((1, aw), lambda i: (0, 0)),
            pl.BlockSpec((n_groups, CHUNK, CHUNK), lambda i: (0, 0, 0)),
            pl.BlockSpec((CHUNK, n_groups), lambda i: (0, 0)),
        ],
        out_specs=out_specs,
        compiler_params=_params("arbitrary"),
        name="spatial",
    )(u, gv, nw.reshape(1, aw), w_s, b_t)


def _conv_body(x_ref, buf_ref, cw_ref, o_ref, cs_ref, *, nb, seq, bw, tc):
    j = pl.program_id(1)
    is_q = j < bw // tc
    is_v = j >= 2 * (bw // tc)
    scale = jnp.where(is_q, HEAD_DIM ** -0.5, 1.0).astype(F32)
    hd = CONV_W - 1
    cw = cw_ref[...]
    for b in range(nb):
        cs_ref[8 - hd:8, :] = buf_ref[b]
        cs_ref[8:8 + seq, :] = x_ref[b * seq:(b + 1) * seq, :]
        acc = cs_ref[8 - hd:8 - hd + seq, :] * cw[0:1]
        for t in range(1, CONV_W):
            acc = acc + cs_ref[8 - hd + t:8 - hd + t + seq, :] * cw[t:t + 1]
        y = _silu(acc)
        parts = []
        for g in range(tc // HEAD_DIM):
            yg = y[:, g * HEAD_DIM:(g + 1) * HEAD_DIM]
            ss = jnp.sum(yg * yg, axis=-1, keepdims=True)
            parts.append(yg * lax.rsqrt(ss + EPS) * scale)
        yn = jnp.concatenate(parts, axis=1) if len(parts) > 1 else parts[0]
        o_ref[b * seq:(b + 1) * seq, :] = jnp.where(is_v, y, yn)


def conv_prep(qkv, buf, conv_w, layer, row0, n_seq, seq, nb, bw):
    _, c3 = qkv.shape
    tc = _tile(bw, 512)
    rows = nb * seq
    assert row0 % rows == 0 and n_seq % nb == 0
    r0 = row0 // rows
    return pl.pallas_call(
        functools.partial(_conv_body, nb=nb, seq=seq, bw=bw, tc=tc),
        out_shape=jax.ShapeDtypeStruct((n_seq * seq, c3), F32),
        grid=(n_seq // nb, c3 // tc),
        in_specs=[
            pl.BlockSpec((rows, tc), lambda i, j: (r0 + i, j)),
            pl.BlockSpec((None, nb, CONV_W - 1, tc), lambda i, j: (layer, i, 0, j)),
            pl.BlockSpec((None, CONV_W, tc), lambda i, j: (layer, 0, j)),
        ],
        out_specs=pl.BlockSpec((rows, tc), lambda i, j: (i, j)),
        scratch_shapes=[pltpu.VMEM((8 + seq, tc), F32)],
        compiler_params=_params("arbitrary", "arbitrary"),
        name="conv_prep",
    )(qkv, buf, conv_w)


def _inv_unit_lower(a, rr, cc, r):
    tr = a.shape[0]
    eye = (rr == cc).astype(F32)

    def blk(b):
        s = b.bit_length() - 1
        return (rr >> s) == (cc >> s)

    b0 = min(16, r)
    n1 = jnp.where(blk(b0), a, 0.0)
    p = eye - n1
    cur = _dot(n1, n1)
    e = 2
    while 2 * e < b0:
        both = _dot(jnp.concatenate([cur, p], axis=0), cur)
        p = p + both[tr:]
        cur = both[:tr]
        e *= 2
    p = p + _dot(p, cur)
    b = b0
    while b < r:
        m = jnp.where(blk(2 * b) & jnp.logical_not(blk(b)), a, 0.0)
        p = p - _dot(p, _dot(m, p))
        b *= 2
    return p


def _delta_body(act_ref, z_ref, bd_ref, s0_ref, alog_ref, dtb_ref, onw_ref, o_ref, s_ref, *, n_heads, r, nsq, tr):
    hd = HEAD_DIM
    bw = n_heads * hd
    upt = tr // r
    n_tiles = nsq * n_heads // upt
    lr = r.bit_length() - 1

    @pl.when(pl.program_id(1) == 0)
    def _():
        s_ref[...] = s0_ref[...]

    alog = alog_ref[...]
    dtb = dtb_ref[...]
    onw = onw_ref[...]
    zf = z_ref[...].astype(F32)
    rowi = lax.broadcasted_iota(jnp.int32, (r, LANES), 0)
    beta_l, gc_l, egc_l, ekd_l, egl_l = [], [], [], [], []
    for sq in range(nsq):
        bd = bd_ref[sq * r:(sq + 1) * r, :]
        gc = -jnp.exp(alog) * _softplus(bd + dtb)
        sh = 1
        while sh < r:
            gc = gc + jnp.where(rowi >= sh, pltpu.roll(gc, sh, axis=0), 0.0)
            sh *= 2
        gl = gc[r - 1:r, :]
        beta_l.append(_sigmoid(bd))
        gc_l.append(gc)
        egc_l.append(jnp.exp(gc))
        ekd_l.append(jnp.exp(gl - gc))
        egl_l.append(jnp.exp(gl))

    rr = lax.broadcasted_iota(jnp.int32, (tr, tr), 0)
    cc = lax.broadcasted_iota(jnp.int32, (tr, tr), 1)
    same = (rr >> lr) == (cc >> lr)
    incl = same & (rr >= cc)
    strict = same & (rr > cc)
    lane = lax.broadcasted_iota(jnp.int32, (tr, LANES), 1)
    unit_row = lax.broadcasted_iota(jnp.int32, (tr, LANES), 0) >> lr
    own_y = (lax.broadcasted_iota(jnp.int32, (tr, upt * hd), 1) >> 7) == (
        lax.broadcasted_iota(jnp.int32, (tr, upt * hd), 0) >> lr)
    own_z = (lax.broadcasted_iota(jnp.int32, (upt * hd, tr), 0) >> 7) == (
        lax.broadcasted_iota(jnp.int32, (upt * hd, tr), 1) >> lr)

    pieces = {}
    for t in range(n_tiles):
        units = [divmod(t * upt + u, n_heads) for u in range(upt)]

        def rows_of(base):
            return jnp.concatenate([act_ref[sq * r:(sq + 1) * r, base + h * hd:base + (h + 1) * hd]
                                    for sq, h in units], axis=0)

        q = rows_of(0)
        k = rows_of(bw)
        v = rows_of(2 * bw)
        zt = jnp.concatenate([zf[sq * r:(sq + 1) * r, h * hd:(h + 1) * hd] for sq, h in units], axis=0)
        h_row = (unit_row + t * upt) & (n_heads - 1)

        def col(tabs, off):
            tab = jnp.concatenate([tabs[sq] for sq, _ in units], axis=0)
            return jnp.sum(jnp.where(lane == h_row + off, tab, 0.0), axis=1, keepdims=True)

        b_col = col(beta_l, 0)
        gc_col = col(gc_l, n_heads)
        egc_col = col(egc_l, n_heads)
        ekd_col = col(ekd_l, n_heads)
        g_i = jnp.broadcast_to(gc_col, (tr, tr))
        g_j = g_i.T
        dec = jnp.where(incl, jnp.exp(jnp.where(incl, g_i - g_j, 0.0)), 0.0)
        kb = k * b_col
        x = lax.dot_general(jnp.concatenate([kb, q], axis=0).astype(BF16), k.astype(BF16),
                            (((1,), (1,)), ((), ())), preferred_element_type=F32)
        a = jnp.where(strict, x[:tr], 0.0) * dec
        qk = x[tr:] * dec
        tinv = _inv_unit_lower(a, rr, cc, r)
        uw = _dot(tinv, jnp.concatenate([v * b_col, kb * egc_col], axis=1))
        u = uw[:, :hd]
        w = uw[:, hd:]
        qd = q * egc_col
        s_units = [s_ref[sq, h] for sq, h in units]
        sst = jnp.concatenate(s_units, axis=0)
        lhs_y = jnp.concatenate([jnp.where(own_y, jnp.tile(w, (1, upt)), 0.0),
                                 jnp.where(own_y, jnp.tile(qd, (1, upt)), 0.0)], axis=0)
        y = _dot(lhs_y, sst)
        v_new = u - y[:tr]
        kdt = (k * ekd_col).T
        lhs_z = jnp.concatenate([qk, jnp.where(own_z, jnp.tile(kdt, (upt, 1)), 0.0)], axis=0)
        zz = _dot(lhs_z, v_new)
        o = y[tr:] + zz[:tr]
        for ui, (sq, h) in enumerate(units):
            s_ref[sq, h] = (s_units[ui] * egl_l[sq][:, n_heads + h:n_heads + h + 1]
                            + zz[tr + ui * hd:tr + (ui + 1) * hd])
        ms = jnp.mean(o * o, axis=-1, keepdims=True)
        out_t = o * lax.rsqrt(ms + EPS) * onw * zt
        for ui, (sq, h) in enumerate(units):
            pieces[(sq, h)] = out_t[ui * r:(ui + 1) * r]

    for h in range(n_heads):
        blk = pieces[(0, h)] if nsq == 1 else jnp.concatenate([pieces[(sq, h)] for sq in range(nsq)], axis=0)
        o_ref[:, h * hd:(h + 1) * hd] = blk.astype(o_ref.dtype)


def delta_rule(act, zs, bd, s0, alog_row, dtb_row, onw, layer, row0, n_seq, seq):
    _, c3 = act.shape
    bw = c3 // 3
    n_heads = bw // HEAD_DIM
    assert n_heads & (n_heads - 1) == 0 and 2 * n_heads <= LANES
    r = min(seq, DELTA_CHUNK)
    assert r & (r - 1) == 0 and r >= 8 and seq % r == 0
    n_chunks = seq // r
    tile_rows = 256
    nsq = max(1, tile_rows // (n_heads * r)) if n_chunks == 1 else 1
    tr = min(tile_rows, nsq * n_heads * r)
    brows = nsq * r
    assert brows % 16 == 0 and row0 % brows == 0 and n_seq % nsq == 0
    assert nsq == 1 or n_chunks == 1
    r0 = row0 // brows
    hd = HEAD_DIM
    if s0.ndim == 5:
        s0_spec = pl.BlockSpec((None, nsq, n_heads, hd, hd), lambda b, c: (layer, b, 0, 0, 0))
    else:
        s0_spec = pl.BlockSpec((nsq, n_heads, hd, hd), lambda b, c: (b, 0, 0, 0))
    rowmap = lambda b, c: (b * n_chunks + c, 0)
    rowmap_t = lambda b, c: (r0 + b * n_chunks + c, 0)
    return pl.pallas_call(
        functools.partial(_delta_body, n_heads=n_heads, r=r, nsq=nsq, tr=tr),
        out_shape=[jax.ShapeDtypeStruct((n_seq * seq, bw), BF16),
                   jax.ShapeDtypeStruct((n_seq, n_heads, hd, hd), F32)],
        grid=(n_seq // nsq, n_chunks),
        in_specs=[
            pl.BlockSpec((brows, c3), rowmap),
            pl.BlockSpec((brows, bw), rowmap_t),
            pl.BlockSpec((brows, LANES), rowmap_t),
            s0_spec,
            pl.BlockSpec((1, LANES), lambda b, c: (0, 0)),
            pl.BlockSpec((1, LANES), lambda b, c: (0, 0)),
            pl.BlockSpec((1, hd), lambda b, c: (0, 0)),
        ],
        out_specs=[pl.BlockSpec((brows, bw), rowmap),
                   pl.BlockSpec((nsq, n_heads, hd, hd), lambda b, c: (b, 0, 0, 0))],
        compiler_params=_params("arbitrary", "arbitrary"),
        name="delta_rule",
    )(act, zs, bd, s0, alog_row, dtb_row, onw)


def _gate_up_body(se_ref, ss_ref, sf_ref, sa_ref, x_ref, wg_ref, wu_ref, o_ref, wgb, wub):
    i = pl.program_id(1)

    @pl.when(sf_ref[i] == 1)
    def _():
        wgb[...] = wg_ref[...].astype(BF16)
        wub[...] = wu_ref[...].astype(BF16)

    @pl.when(sa_ref[i] == 1)
    def _():
        x = x_ref[...]
        a = jnp.dot(x, wgb[...], preferred_element_type=F32)
        b = jnp.dot(x, wub[...], preferred_element_type=F32)
        o_ref[...] = (_silu(a) * b).astype(o_ref.dtype)

    @pl.when(sa_ref[i] == 0)
    def _():
        o_ref[...] = jnp.zeros_like(o_ref)


def gate_up(xs, wg, wu, widx, plan, tm):
    p, k = xs.shape
    f = wg.shape[-1]
    tn = _tile(f, 512)
    n_tiles = p // tm
    grid_spec = pltpu.PrefetchScalarGridSpec(
        num_scalar_prefetch=4,
        grid=(f // tn, n_tiles),
        in_specs=[
            pl.BlockSpec((tm, k), lambda j, i, se, ss, sf, sa: (ss[i], 0)),
            pl.BlockSpec((None, None, k, tn), lambda j, i, se, ss, sf, sa: (widx, se[i], 0, j)),
            pl.BlockSpec((None, None, k, tn), lambda j, i, se, ss, sf, sa: (widx, se[i], 0, j)),
        ],
        out_specs=pl.BlockSpec((tm, tn), lambda j, i, se, ss, sf, sa: (i, j)),
        scratch_shapes=[pltpu.VMEM((k, tn), BF16), pltpu.VMEM((k, tn), BF16)],
    )
    return pl.pallas_call(
        _gate_up_body,
        out_shape=jax.ShapeDtypeStruct((p, f), BF16),
        grid_spec=grid_spec,
        compiler_params=_params("arbitrary", "arbitrary"),
        name="ffn_gate_up",
    )(plan["sub_expert"], plan["sub_src"], plan["sub_first"], plan["sub_active"], xs, wg, wu)


def _down_body(be_ref, bs_ref, ba_ref, sa_ref, h_ref, w_ref, *rest, nk, nsub, sub, residual):
    if residual:
        x_ref, o_ref, acc = rest
    else:
        o_ref, acc = rest
    b = pl.program_id(1)
    kk = pl.program_id(2)
    wb = w_ref[...].astype(BF16)
    for s in range(nsub):
        rs = slice(s * sub, (s + 1) * sub)
        active = sa_ref[b * nsub + s] == 1

        @pl.when(active)
        def _():
            @pl.when(kk == 0)
            def _():
                acc[rs, :] = jnp.zeros((sub, acc.shape[1]), F32)

            acc[rs, :] += jnp.dot(h_ref[rs, :], wb, preferred_element_type=F32)

            @pl.when(kk == nk - 1)
            def _():
                if residual:
                    o_ref[rs, :] = x_ref[rs, :] + acc[rs, :]
                else:
                    o_ref[rs, :] = acc[rs, :]

        @pl.when(jnp.logical_not(active) & (kk == 0))
        def _():
            o_ref[rs, :] = jnp.zeros((sub, o_ref.shape[1]), o_ref.dtype)


def down(h, wd, widx, plan, big, sub, x_res=None):
    p, f = h.shape
    d = wd.shape[-1]
    tn = _tile(d, 1024)
    tk = _tile(f, 1024)
    nk = f // tk
    n_big = p // big
    nsub = big // sub

    def kmap(b, kk, ba):
        return jnp.where(ba[b] == 1, kk, nk - 1)

    in_specs = [
        pl.BlockSpec((big, tk), lambda n, b, kk, be, bs, ba, sa: (bs[b], kmap(b, kk, ba))),
        pl.BlockSpec((None, None, tk, tn), lambda n, b, kk, be, bs, ba, sa: (widx, be[b], kmap(b, kk, ba), n)),
    ]
    args = [plan["big_expert"], plan["big_src"], plan["big_active"], plan["sub_active"], h, wd]
    if x_res is not None:
        in_specs.append(pl.BlockSpec((big, tn), lambda n, b, kk, be, bs, ba, sa: (b, n)))
        args.append(x_res)
    grid_spec = pltpu.PrefetchScalarGridSpec(
        num_scalar_prefetch=4,
        grid=(d // tn, n_big, nk),
        in_specs=in_specs,
        out_specs=pl.BlockSpec((big, tn), lambda n, b, kk, be, bs, ba, sa: (b, n)),
        scratch_shapes=[pltpu.VMEM((big, tn), F32)],
    )
    return pl.pallas_call(
        functools.partial(_down_body, nk=nk, nsub=nsub, sub=sub, residual=x_res is not None),
        out_shape=jax.ShapeDtypeStruct((p, d), F32),
        grid_spec=grid_spec,
        compiler_params=_params("arbitrary", "arbitrary", "arbitrary"),
        name="ffn_down",
    )(*args)


def _router_body(x_ref, nw_ref, r_ref, xn_ref, comb_ref, sel_ref, *, n_experts):
    x = x_ref[...]
    ms = jnp.mean(x * x, axis=-1, keepdims=True)
    xn = x * lax.rsqrt(ms + EPS) * nw_ref[...]
    xn_ref[...] = xn
    logits = jnp.dot(xn, r_ref[...], preferred_element_type=F32, precision=lax.Precision.HIGHEST)
    lane = lax.broadcasted_iota(jnp.int32, logits.shape, 1)
    neg = jnp.float32(-jnp.inf)
    lg = jnp.where(lane < n_experts, logits, neg)
    m1 = jnp.max(lg, axis=-1, keepdims=True)
    i1 = jnp.min(jnp.where(lg == m1, lane, LANES), axis=-1, keepdims=True)
    lg2 = jnp.where(lane == i1, neg, lg)
    m2 = jnp.max(lg2, axis=-1, keepdims=True)
    i2 = jnp.min(jnp.where(lg2 == m2, lane, LANES), axis=-1, keepdims=True)
    e2 = jnp.exp(m2 - m1)
    den = 1.0 + e2
    comb_ref[...] = jnp.where(lane == i1, 1.0 / den, 0.0) + jnp.where(lane == i2, e2 / den, 0.0)
    sel_ref[...] = ((lane == i1) | (lane == i2)).astype(jnp.int32)


def router_topk(x, nw, router_pad, n_experts):
    m, d = x.shape
    tm = _tile(m, 256)
    return pl.pallas_call(
        functools.partial(_router_body, n_experts=n_experts),
        out_shape=[jax.ShapeDtypeStruct((m, d), F32),
                   jax.ShapeDtypeStruct((m, LANES), F32),
                   jax.ShapeDtypeStruct((m, LANES), jnp.int32)],
        grid=(m // tm,),
        in_specs=[pl.BlockSpec((tm, d), lambda i: (i, 0)),
                  pl.BlockSpec((1, d), lambda i: (0, 0)),
                  pl.BlockSpec((d, LANES), lambda i: (0, 0))],
        out_specs=[pl.BlockSpec((tm, d), lambda i: (i, 0)),
                   pl.BlockSpec((tm, LANES), lambda i: (i, 0)),
                   pl.BlockSpec((tm, LANES), lambda i: (i, 0))],
        compiler_params=_params("arbitrary"),
        name="router_topk",
    )(x, nw.reshape(1, d), router_pad)


def _row_copy(src_hbm, row, dst, r, sem):
    return pltpu.make_async_copy(src_hbm.at[pl.ds(row, 1), :], dst.at[pl.ds(r, 1), :], sem)


def _gather_body(idx_ref, act_ref, x_hbm, o_ref, buf, sem, *, rows):
    g = pl.program_id(0)
    base = g * rows

    @pl.when(act_ref[g] == 1)
    def _():
        def start(r, carry):
            _row_copy(x_hbm, idx_ref[base + r], buf, r, sem).start()
            return carry

        lax.fori_loop(0, rows, start, 0)

        def wait(r, carry):
            _row_copy(x_hbm, 0, buf, r, sem).wait()
            return carry

        lax.fori_loop(0, rows, wait, 0)
        o_ref[...] = buf[...].astype(o_ref.dtype)

    @pl.when(act_ref[g] == 0)
    def _():
        o_ref[...] = jnp.zeros_like(o_ref)


def gather_rows(x, src_rows, blk_active, rows, out_dtype):
    _, d = x.shape
    p = src_rows.shape[0]
    grid_spec = pltpu.PrefetchScalarGridSpec(
        num_scalar_prefetch=2,
        grid=(p // rows,),
        in_specs=[pl.BlockSpec(memory_space=pl.ANY)],
        out_specs=pl.BlockSpec((rows, d), lambda i, idx, act: (i, 0)),
        scratch_shapes=[pltpu.VMEM((rows, d), x.dtype), pltpu.SemaphoreType.DMA],
    )
    return pl.pallas_call(
        functools.partial(_gather_body, rows=rows),
        out_shape=jax.ShapeDtypeStruct((p, d), out_dtype),
        grid_spec=grid_spec,
        compiler_params=_params("arbitrary"),
        name="moe_gather",
    )(src_rows, blk_active, x)


def _combine_body(p0_ref, p1_ref, x_ref, g0_ref, g1_ref, o_hbm, y_ref, buf_a, buf_b, sems, *, rows):
    base = pl.program_id(0) * rows

    def start(r, carry):
        _row_copy(o_hbm, p0_ref[base + r], buf_a, r, sems.at[0]).start()
        _row_copy(o_hbm, p1_ref[base + r], buf_b, r, sems.at[1]).start()
        return carry

    lax.fori_loop(0, rows, start, 0)

    def wait(r, carry):
        _row_copy(o_hbm, 0, buf_a, r, sems.at[0]).wait()
        _row_copy(o_hbm, 0, buf_b, r, sems.at[1]).wait()
        return carry

    lax.fori_loop(0, rows, wait, 0)
    g0 = g0_ref[...]
    g1 = g1_ref[...]
    for cblk in range(x_ref.shape[1] // LANES):
        cs = slice(cblk * LANES, (cblk + 1) * LANES)
        y_ref[:, cs] = x_ref[:, cs] + g0 * buf_a[:, cs] + g1 * buf_b[:, cs]


def combine(x, o_rows, pos0, pos1, g0b, g1b):
    m, d = x.shape
    rows = _tile(m, 256)
    grid_spec = pltpu.PrefetchScalarGridSpec(
        num_scalar_prefetch=2,
        grid=(m // rows,),
        in_specs=[pl.BlockSpec((rows, d), lambda i, a, b: (i, 0)),
                  pl.BlockSpec((rows, LANES), lambda i, a, b: (i, 0)),
                  pl.BlockSpec((rows, LANES), lambda i, a, b: (i, 0)),
                  pl.BlockSpec(memory_space=pl.ANY)],
        out_specs=pl.BlockSpec((rows, d), lambda i, a, b: (i, 0)),
        scratch_shapes=[pltpu.VMEM((rows, d), F32), pltpu.VMEM((rows, d), F32),
                        pltpu.SemaphoreType.DMA((2,))],
    )
    return pl.pallas_call(
        functools.partial(_combine_body, rows=rows),
        out_shape=jax.ShapeDtypeStruct((m, d), F32),
        grid_spec=grid_spec,
        compiler_params=_params("arbitrary"),
        name="moe_combine",
    )(pos0, pos1, x, g0b, g1b, o_rows)


def _tile_plan(counts, row_start, big_expert, n_big_active, n_big, big, sub):
    nsub = big // sub
    i32 = jnp.int32
    bt = jnp.arange(n_big, dtype=i32)
    big_active = (bt < n_big_active).astype(i32)
    big_src = jnp.minimum(bt, n_big_active - 1).astype(i32)
    st = jnp.arange(n_big * nsub, dtype=i32)
    st_expert = big_expert[st // nsub]
    valid_end = row_start[st_expert] + counts[st_expert]
    sub_active = ((st * sub < valid_end) & (st // nsub < n_big_active)).astype(i32)
    sub_src = jnp.maximum(lax.cummax(jnp.where(sub_active == 1, st, -1), axis=0), 0).astype(i32)
    sub_expert = st_expert[sub_src]
    prev = jnp.concatenate([jnp.full((1,), -1, i32), sub_expert[:-1]])
    sub_first = (sub_expert != prev).astype(i32)
    return dict(big_expert=big_expert.astype(i32), big_src=big_src, big_active=big_active,
                sub_expert=sub_expert.astype(i32), sub_src=sub_src, sub_first=sub_first, sub_active=sub_active)


def _dispatch_plan(sel, comb, n_experts, big, sub):
    t = sel.shape[0]
    n_big = (2 * t) // big + n_experts
    p_rows = n_big * big
    sel_e = sel[:, :n_experts]
    csum = jnp.cumsum(sel_e, axis=0)
    counts = csum[-1]
    rank = csum - sel_e
    btiles = (counts + big - 1) // big
    big_end = jnp.cumsum(btiles)
    row_start = (big_end - btiles) * big
    pos = row_start[None, :] + rank
    n_big_active = big_end[-1]
    bt = jnp.minimum(jnp.arange(n_big, dtype=jnp.int32), n_big_active - 1)
    big_expert = jnp.minimum(jnp.searchsorted(big_end, bt, side="right"), n_experts - 1)
    plan = _tile_plan(counts, row_start, big_expert, n_big_active, n_big, big, sub)
    tok = jnp.broadcast_to(jnp.arange(t, dtype=jnp.int32)[:, None], pos.shape)
    flat_pos = jnp.where(sel_e > 0, pos, p_rows).reshape(-1)
    src_rows = jnp.zeros((p_rows,), jnp.int32).at[flat_pos].set(tok.reshape(-1), mode="drop")
    far = jnp.int32(2**30)
    lo = jnp.where(sel_e > 0, pos, far)
    hi = jnp.where(sel_e > 0, pos, -1)
    pos_lo = jnp.min(lo, axis=1).astype(jnp.int32)
    pos_hi = jnp.max(hi, axis=1).astype(jnp.int32)
    comb_e = comb[:, :n_experts]
    g_lo = jnp.take_along_axis(comb_e, jnp.argmin(lo, axis=1)[:, None], axis=1)
    g_hi = jnp.take_along_axis(comb_e, jnp.argmax(hi, axis=1)[:, None], axis=1)
    return plan, src_rows, pos_lo, pos_hi, jnp.broadcast_to(g_lo, (t, LANES)), jnp.broadcast_to(g_hi, (t, LANES))


def _dense_plan(t, big, sub):
    n_big = t // big
    z = jnp.zeros((1,), jnp.int32)
    return _tile_plan(jnp.full((1,), t, jnp.int32), z, jnp.zeros((n_big,), jnp.int32), jnp.int32(n_big), n_big, big, sub)


def kernel(x_prompt, x_sample, state_delta, state_conv, norm_mix, norm_ffn, norm_final, w_in, conv_w, a_log, dt_bias, o_norm_w, sgu_norm_w, w_spatial, b_spatial, w_proj_a, w_proj_b, w_out, ffn_w_gate, ffn_w_up, ffn_w_down, router, moe_w_gate, moe_w_up, moe_w_down):
    bp, lp, d = x_prompt.shape
    bs, ls, _ = x_sample.shape
    depth = norm_mix.shape[0]
    n_groups = w_spatial.shape[1]
    aw = sgu_norm_w.shape[1]
    n_heads = a_log.shape[1]
    bw = n_heads * HEAD_DIM
    qkv_dim = 3 * bw
    n_experts = router.shape[2]
    tp, ts = bp * lp, bs * ls
    t = tp + ts
    c2 = 2 * aw
    c3 = c2 + qkv_dim
    c4 = c3 + bw
    c6 = c4 + 2 * n_heads
    assert lp % CHUNK == 0 and ls <= CHUNK and CHUNK % ls == 0 and ls >= CONV_W - 1
    assert c4 % LANES == 0 and w_in.shape[2] == c6 + 2 * d

    x = jnp.concatenate([x_prompt.reshape(tp, d), x_sample.reshape(ts, d)], axis=0)
    w_gate = w_in[:, :, c6:]
    zero_conv = jnp.zeros((depth, bp, CONV_W - 1, qkv_dim), F32)
    zero_delta = jnp.zeros((bp, n_heads, HEAD_DIM, HEAD_DIM), F32)
    rep = CHUNK // ls
    eye_rep = jnp.eye(rep, dtype=F32)
    lane_pad = LANES - 2 * n_heads

    tail = jnp.arange(CONV_W - 1, dtype=jnp.int32)
    tail_p = (jnp.arange(bp, dtype=jnp.int32)[:, None] * lp + (lp - (CONV_W - 1)) + tail[None, :]).reshape(-1)
    tail_s = (tp + jnp.arange(bs, dtype=jnp.int32)[:, None] * ls + (ls - (CONV_W - 1)) + tail[None, :]).reshape(-1)
    tm_ffn = _tile(t, 512)
    big_dense = _tile(t, 1024)
    big_moe = 4 * tm_ffn
    new_conv_p, new_conv_s, new_delta_p, new_delta_s, chunk_v = [], [], [], [], []
    for l in range(depth):
        xn = rmsnorm(x, norm_mix[l], BF16)
        u = proj(xn, w_in, l, 0, aw, _gelu, BF16)
        gv = proj(xn, w_in, l, aw, aw, _gelu, F32)
        qkv = proj(xn, w_in, l, c2, qkv_dim, _ident, F32)
        zs = proj(xn, w_in, l, c3, bw, _silu, BF16)
        bd = proj(xn, w_in, l, c4, LANES, _ident, F32, tn_pref=LANES)
        gates = proj(xn, w_gate, l, 0, 2 * d, _sigmoid, BF16)

        w_s = w_spatial[l]
        b_t = b_spatial[l].T
        w_s_small = jnp.einsum("ab,gts->gatbs", eye_rep, w_s[:, :ls, :ls]).reshape(n_groups, CHUNK, CHUNK)
        b_t_small = jnp.tile(b_spatial[l][:, :ls], (1, rep)).T
        (a_p,) = spatial(u, gv, sgu_norm_w[l], w_s, b_t, 0, tp, False)
        a_s, v_s = spatial(u, gv, sgu_norm_w[l], w_s_small, b_t_small, tp, ts, True)
        chunk_v.append(v_s.reshape(bs, ls, aw))

        nb_s = _tile(bs, max(1, 128 // ls))
        act_p = conv_prep(qkv, zero_conv, conv_w, l, 0, bp, lp, 1, bw)
        act_s = conv_prep(qkv, state_conv, conv_w, l, tp, bs, ls, nb_s, bw)
        alog_row = jnp.pad(a_log[l], (n_heads, lane_pad)).reshape(1, LANES)
        dtb_row = jnp.pad(dt_bias[l], (n_heads, lane_pad)).reshape(1, LANES)
        onw = o_norm_w[l].reshape(1, HEAD_DIM)
        b_p, s_p = delta_rule(act_p, zs, bd, zero_delta, alog_row, dtb_row, onw, l, 0, bp, lp)
        b_s, s_s = delta_rule(act_s, zs, bd, state_delta, alog_row, dtb_row, onw, l, tp, bs, ls)
        new_delta_p.append(s_p)
        new_delta_s.append(s_s)
        new_conv_p.append(jnp.take(qkv, tail_p, axis=0).reshape(bp, CONV_W - 1, qkv_dim))
        new_conv_s.append(jnp.take(qkv, tail_s, axis=0).reshape(bs, CONV_W - 1, qkv_dim))

        mg = merge(a_p, a_s, b_p, b_s, w_proj_a, w_proj_b, gates, l)
        x = resproj(mg, w_out, l, x)

        j = l // 2
        if l % 2 == 0:
            xf = rmsnorm(x, norm_ffn[l], BF16)
            plan = _dense_plan(t, big_dense, tm_ffn)
            h = gate_up(xf, ffn_w_gate[:, None], ffn_w_up[:, None], j, plan, tm_ffn)
            x = down(h, ffn_w_down[:, None], j, plan, big_dense, big_dense, x_res=x)
        else:
            router_pad = jnp.pad(router[j], ((0, 0), (0, LANES - n_experts)))
            xf32, comb, sel = router_topk(x, norm_ffn[l], router_pad, n_experts)
            plan, src_rows, p0, p1, g0b, g1b = _dispatch_plan(sel, comb, n_experts, big_moe, tm_ffn)
            g_rows = _tile(tm_ffn, 256)
            blk_active = jnp.repeat(plan["sub_active"], tm_ffn // g_rows)
            xs = gather_rows(xf32, src_rows, blk_active, g_rows, BF16)
            h = gate_up(xs, moe_w_gate, moe_w_up, j, plan, tm_ffn)
            o_rows = down(h, moe_w_down, j, plan, big_moe, tm_ffn)
            x = combine(x, o_rows, p0, p1, g0b, g1b)

    y_p, y_s = rmsnorm_split(x, norm_final, tp)
    y_prompt = y_p.reshape(bp, lp, d)
    y_sample = y_s.reshape(bs, ls, d)
    return (y_prompt, y_sample, jnp.stack(new_delta_p), jnp.stack(new_conv_p),
            jnp.stack(new_delta_s), jnp.stack(new_conv_s), jnp.stack(chunk_v))
```

```python
import functools

import jax
import jax.numpy as jnp
from jax import lax
from jax.experimental import pallas as pl
from jax.experimental.pallas import tpu as pltpu

F32 = jnp.float32
BF16 = jnp.bfloat16

HEAD_DIM = 128
CHUNK = 128
DELTA_CHUNK = 64
CONV_W = 4
EPS = 1e-6

LANES = 128
V7X_VMEM_BYTES = 64 * 2**20
VMEM_LIMIT = V7X_VMEM_BYTES - 8 * 2**20


def _params(*sem):
    return pltpu.CompilerParams(dimension_semantics=sem, vmem_limit_bytes=VMEM_LIMIT)


def _tile(n, pref):
    t = pref
    while t > 8 and n % t:
        t //= 2
    assert n % t == 0, (n, pref)
    return t


def _dot(a, b):
    return jnp.dot(a.astype(BF16), b.astype(BF16), preferred_element_type=F32)


def _softplus(x):
    return jnp.maximum(x, 0.0) + jnp.log1p(jnp.exp(-jnp.abs(x)))


def _sigmoid(x):
    return jax.nn.sigmoid(x)


def _silu(x):
    return x * jax.nn.sigmoid(x)


def _gelu(x):
    return jax.nn.gelu(x)


def _ident(x):
    return x


def _rmsnorm_body(x_ref, w_ref, o_ref):
    x = x_ref[...].astype(F32)
    ms = jnp.mean(x * x, axis=-1, keepdims=True)
    o_ref[...] = (x * lax.rsqrt(ms + EPS) * w_ref[...]).astype(o_ref.dtype)


def rmsnorm(x, w, out_dtype):
    m, d = x.shape
    tm = _tile(m, 256)
    return pl.pallas_call(
        _rmsnorm_body,
        out_shape=jax.ShapeDtypeStruct((m, d), out_dtype),
        grid=(m // tm,),
        in_specs=[pl.BlockSpec((tm, d), lambda i: (i, 0)), pl.BlockSpec((1, d), lambda i: (0, 0))],
        out_specs=pl.BlockSpec((tm, d), lambda i: (i, 0)),
        compiler_params=_params("arbitrary"),
        name="rmsnorm",
    )(x, w.reshape(1, d).astype(F32))


def _rmsnorm_split_body(x_ref, w_ref, op_ref, os_ref, *, n_p):
    x = x_ref[...]
    ms = jnp.mean(x * x, axis=-1, keepdims=True)
    y = x * lax.rsqrt(ms + EPS) * w_ref[...]
    i = pl.program_id(0)

    @pl.when(i < n_p)
    def _():
        op_ref[...] = y

    @pl.when(i >= n_p)
    def _():
        os_ref[...] = y


def rmsnorm_split(x, w, tp):
    m, d = x.shape
    ts = m - tp
    tm = _tile(ts, 256)
    assert tp % tm == 0
    n_p = tp // tm
    return pl.pallas_call(
        functools.partial(_rmsnorm_split_body, n_p=n_p),
        out_shape=[jax.ShapeDtypeStruct((tp, d), F32), jax.ShapeDtypeStruct((ts, d), F32)],
        grid=(m // tm,),
        in_specs=[pl.BlockSpec((tm, d), lambda i: (i, 0)), pl.BlockSpec((1, d), lambda i: (0, 0))],
        out_specs=[pl.BlockSpec((tm, d), lambda i: (jnp.minimum(i, n_p - 1), 0)),
                   pl.BlockSpec((tm, d), lambda i: (jnp.maximum(i - n_p, 0), 0))],
        compiler_params=_params("arbitrary"),
        name="rmsnorm_split",
    )(x, w.reshape(1, d).astype(F32))


def _proj_body(x_ref, w_ref, o_ref, *, act):
    acc = jnp.dot(x_ref[...], w_ref[...].astype(BF16), preferred_element_type=F32)
    o_ref[...] = act(acc).astype(o_ref.dtype)


def proj(x, w, layer, col0, n, act, out_dtype, tn_pref=512):
    m, k = x.shape
    tm = _tile(m, 1024)
    tn = _tile(n, tn_pref)
    assert col0 % tn == 0
    off = col0 // tn
    return pl.pallas_call(
        functools.partial(_proj_body, act=act),
        out_shape=jax.ShapeDtypeStruct((m, n), out_dtype),
        grid=(m // tm, n // tn),
        in_specs=[
            pl.BlockSpec((tm, k), lambda i, j: (i, 0)),
            pl.BlockSpec((None, k, tn), lambda i, j: (layer, 0, off + j)),
        ],
        out_specs=pl.BlockSpec((tm, tn), lambda i, j: (i, j)),
        compiler_params=_params("arbitrary", "arbitrary"),
        name="proj",
    )(x, w)


def _merge_body(ap_ref, as_ref, bp_ref, bs_ref, wa_ref, wb_ref, ga_ref, gax_ref, gb_ref, gbx_ref, gt_ref, o_ref,
                *, n_p, sh):
    is_p = pl.program_id(0) < n_p
    last = pl.program_id(1) == pl.num_programs(1) - 1
    a = jnp.where(is_p, ap_ref[...], as_ref[...])
    b = jnp.where(is_p, bp_ref[...], bs_ref[...])
    pa = jnp.dot(a, wa_ref[...].astype(BF16), preferred_element_type=F32)
    pb = jnp.dot(b, wb_ref[...].astype(BF16), preferred_element_type=F32)
    tn = o_ref.shape[1]

    def shifted(main, extra):
        g = jnp.concatenate([main.astype(F32), extra.astype(F32)], axis=1)
        return g[:, sh:sh + tn]

    ga = shifted(ga_ref[...], gax_ref[...])
    gb = shifted(gb_ref[...], jnp.where(last, gt_ref[...], gbx_ref[...]))
    o_ref[...] = (ga * pa + gb * pb).astype(o_ref.dtype)


def merge(a_p, a_s, b_p, b_s, wa, wb, g_main, g_tail, sh, layer):
    tp, ka = a_p.shape
    ts = a_s.shape[0]
    kb = b_p.shape[1]
    d = wa.shape[2]
    tm = _tile(ts, 512)
    assert tp % tm == 0
    n_p, n_s = tp // tm, ts // tm
    tn = _tile(d, 512)
    nb = d // tn
    lb = tn // LANES
    n_lane_blk = 2 * d // LANES
    pmap = lambda i, j: (jnp.minimum(i, n_p - 1), 0)
    smap = lambda i, j: (jnp.maximum(i - n_p, 0), 0)
    return pl.pallas_call(
        functools.partial(_merge_body, n_p=n_p, sh=sh),
        out_shape=jax.ShapeDtypeStruct((tp + ts, d), BF16),
        grid=(n_p + n_s, nb),
        in_specs=[
            pl.BlockSpec((tm, ka), pmap),
            pl.BlockSpec((tm, ka), smap),
            pl.BlockSpec((tm, kb), pmap),
            pl.BlockSpec((tm, kb), smap),
            pl.BlockSpec((None, ka, tn), lambda i, j: (layer, 0, j)),
            pl.BlockSpec((None, kb, tn), lambda i, j: (layer, 0, j)),
            pl.BlockSpec((tm, tn), lambda i, j: (i, j)),
            pl.BlockSpec((tm, LANES), lambda i, j: (i, (j + 1) * lb)),
            pl.BlockSpec((tm, tn), lambda i, j: (i, nb + j)),
            pl.BlockSpec((tm, LANES), lambda i, j: (i, jnp.minimum((nb + j + 1) * lb, n_lane_blk - 1))),
            pl.BlockSpec((tm, LANES), lambda i, j: (i, 0)),
        ],
        out_specs=pl.BlockSpec((tm, tn), lambda i, j: (i, j)),
        compiler_params=_params("arbitrary", "arbitrary"),
        name="merge",
    )(a_p, a_s, b_p, b_s, wa, wb, g_main, g_main, g_main, g_main, g_tail)


def _resproj_body(m_ref, w_ref, x_ref, o_ref):
    acc = jnp.dot(m_ref[...], w_ref[...].astype(BF16), preferred_element_type=F32)
    o_ref[...] = x_ref[...] + acc


def resproj(mg, w, layer, x):
    m, k = mg.shape
    d = w.shape[2]
    tm = _tile(m, 1024)
    tn = _tile(d, 512)
    return pl.pallas_call(
        _resproj_body,
        out_shape=jax.ShapeDtypeStruct((m, d), F32),
        grid=(m // tm, d // tn),
        in_specs=[
            pl.BlockSpec((tm, k), lambda i, j: (i, 0)),
            pl.BlockSpec((None, k, tn), lambda i, j: (layer, 0, j)),
            pl.BlockSpec((tm, tn), lambda i, j: (i, j)),
        ],
        out_specs=pl.BlockSpec((tm, tn), lambda i, j: (i, j)),
        compiler_params=_params("arbitrary", "arbitrary"),
        name="resproj",
    )(mg, w, x)


def _spatial_body(u_ref, gv_ref, nw_ref, w_ref, bt_ref, a_ref, *v_out, n_groups, rows):
    gv = gv_ref[...]
    ms = jnp.mean(gv * gv, axis=-1, keepdims=True)
    v = gv * lax.rsqrt(ms + EPS) * nw_ref[...]
    if v_out:
        v_out[0][...] = v
    gd = gv.shape[1] // n_groups
    rr = lax.broadcasted_iota(jnp.int32, (CHUNK, CHUNK), 0)
    cc = lax.broadcasted_iota(jnp.int32, (CHUNK, CHUNK), 1)
    tril = rr >= cc
    for g in range(n_groups):
        wm = jnp.where(tril, w_ref[g], 0.0).astype(BF16)
        bcol = bt_ref[:, g:g + 1]
        for c in range(rows // CHUNK):
            rs = slice(c * CHUNK, (c + 1) * CHUNK)
            cs = slice(g * gd, (g + 1) * gd)
            s = jnp.dot(wm, v[rs, cs].astype(BF16), preferred_element_type=F32) + bcol
            a_ref[rs, cs] = (u_ref[rs, cs].astype(F32) * s).astype(a_ref.dtype)


def spatial(u, gv, nw, w_s, b_t, row0, nrows, write_v):
    _, aw = u.shape
    n_groups = w_s.shape[0]
    rows = _tile(nrows, 256)
    assert rows % CHUNK == 0 and row0 % rows == 0
    r0 = row0 // rows
    out_shape = [jax.ShapeDtypeStruct((nrows, aw), BF16)]
    out_specs = [pl.BlockSpec((rows, aw), lambda i: (i, 0))]
    if write_v:
        out_shape.append(jax.ShapeDtypeStruct((nrows, aw), F32))
        out_specs.append(pl.BlockSpec((rows, aw), lambda i: (i, 0)))
    return pl.pallas_call(
        functools.partial(_spatial_body, n_groups=n_groups, rows=rows),
        out_shape=out_shape,
        grid=(nrows // rows,),
        in_specs=[
            pl.BlockSpec((rows, aw), lambda i: (r0 + i, 0)),
            pl.BlockSpec((rows, aw), lambda i: (r0 + i, 0)),
            pl.BlockSpec((1, aw), lambda i: (0, 0)),
            pl.BlockSpec((n_groups, CHUNK, CHUNK), lambda i: (0, 0, 0)),
            pl.BlockSpec((CHUNK, n_groups), lambda i: (0, 0)),
        ],
        out_specs=out_specs,
        compiler_params=_params("arbitrary"),
        name="spatial",
    )(u, gv, nw.reshape(1, aw), w_s, b_t)


def _conv_body(x_ref, buf_ref, cw_ref, o_ref, cs_ref, *, nb, seq, bw, tc):
    j = pl.program_id(1)
    is_q = j < bw // tc
    is_v = j >= 2 * (bw // tc)
    scale = jnp.where(is_q, HEAD_DIM ** -0.5, 1.0).astype(F32)
    hd = CONV_W - 1
    cw = cw_ref[...]
    for b in range(nb):
        cs_ref[8 - hd:8, :] = buf_ref[b]
        cs_ref[8:8 + seq, :] = x_ref[b * seq:(b + 1) * seq, :]
        acc = cs_ref[8 - hd:8 - hd + seq, :] * cw[0:1]
        for t in range(1, CONV_W):
            acc = acc + cs_ref[8 - hd + t:8 - hd + t + seq, :] * cw[t:t + 1]
        y = _silu(acc)
        parts = []
        for g in range(tc // HEAD_DIM):
            yg = y[:, g * HEAD_DIM:(g + 1) * HEAD_DIM]
            ss = jnp.sum(yg * yg, axis=-1, keepdims=True)
            parts.append(yg * lax.rsqrt(ss + EPS) * scale)
        yn = jnp.concatenate(parts, axis=1) if len(parts) > 1 else parts[0]
        o_ref[b * seq:(b + 1) * seq, :] = jnp.where(is_v, y, yn)


def conv_prep(qkv, buf, conv_w, layer, row0, n_seq, seq, nb, bw):
    _, c3 = qkv.shape
    tc = _tile(bw, 512)
    rows = nb * seq
    assert row0 % rows == 0 and n_seq % nb == 0
    r0 = row0 // rows
    return pl.pallas_call(
        functools.partial(_conv_body, nb=nb, seq=seq, bw=bw, tc=tc),
        out_shape=jax.ShapeDtypeStruct((n_seq * seq, c3), F32),
        grid=(n_seq // nb, c3 // tc),
        in_specs=[
            pl.BlockSpec((rows, tc), lambda i, j: (r0 + i, j)),
            pl.BlockSpec((None, nb, CONV_W - 1, tc), lambda i, j: (layer, i, 0, j)),
            pl.BlockSpec((None, CONV_W, tc), lambda i, j: (layer, 0, j)),
        ],
        out_specs=pl.BlockSpec((rows, tc), lambda i, j: (i, j)),
        scratch_shapes=[pltpu.VMEM((8 + seq, tc), F32)],
        compiler_params=_params("arbitrary", "arbitrary"),
        name="conv_prep",
    )(qkv, buf, conv_w)


def _inv_unit_lower(a, rr, cc, r):
    tr = a.shape[0]
    eye = (rr == cc).astype(F32)

    def blk(b):
        s = b.bit_length() - 1
        return (rr >> s) == (cc >> s)

    b0 = min(16, r)
    n1 = jnp.where(blk(b0), a, 0.0)
    p = eye - n1
    cur = _dot(n1, n1)
    e = 2
    while 2 * e < b0:
        both = _dot(jnp.concatenate([cur, p], axis=0), cur)
        p = p + both[tr:]
        cur = both[:tr]
        e *= 2
    p = p + _dot(p, cur)
    b = b0
    while b < r:
        m = jnp.where(blk(2 * b) & jnp.logical_not(blk(b)), a, 0.0)
        p = p - _dot(p, _dot(m, p))
        b *= 2
    return p


def _delta_body(act_ref, z_ref, bd_ref, s0_ref, alog_ref, dtb_ref, onw_ref, o_ref, s_ref, *, n_heads, r, nsq, tr):
    hd = HEAD_DIM
    bw = n_heads * hd
    upt = tr // r
    n_tiles = nsq * n_heads // upt
    lr = r.bit_length() - 1

    @pl.when(pl.program_id(1) == 0)
    def _():
        s_ref[...] = s0_ref[...]

    alog = alog_ref[...]
    dtb = dtb_ref[...]
    onw = onw_ref[...]
    zf = z_ref[...].astype(F32)
    rowi = lax.broadcasted_iota(jnp.int32, (r, LANES), 0)
    beta_l, gc_l, egc_l, ekd_l, egl_l = [], [], [], [], []
    for sq in range(nsq):
        bd = bd_ref[sq * r:(sq + 1) * r, :]
        gc = -jnp.exp(alog) * _softplus(bd + dtb)
        sh = 1
        while sh < r:
            gc = gc + jnp.where(rowi >= sh, pltpu.roll(gc, sh, axis=0), 0.0)
            sh *= 2
        gl = gc[r - 1:r, :]
        beta_l.append(_sigmoid(bd))
        gc_l.append(gc)
        egc_l.append(jnp.exp(gc))
        ekd_l.append(jnp.exp(gl - gc))
        egl_l.append(jnp.exp(gl))

    rr = lax.broadcasted_iota(jnp.int32, (tr, tr), 0)
    cc = lax.broadcasted_iota(jnp.int32, (tr, tr), 1)
    same = (rr >> lr) == (cc >> lr)
    incl = same & (rr >= cc)
    strict = same & (rr > cc)
    lane = lax.broadcasted_iota(jnp.int32, (tr, LANES), 1)
    unit_row = lax.broadcasted_iota(jnp.int32, (tr, LANES), 0) >> lr
    own_y = (lax.broadcasted_iota(jnp.int32, (tr, upt * hd), 1) >> 7) == (
        lax.broadcasted_iota(jnp.int32, (tr, upt * hd), 0) >> lr)
    own_z = (lax.broadcasted_iota(jnp.int32, (upt * hd, tr), 0) >> 7) == (
        lax.broadcasted_iota(jnp.int32, (upt * hd, tr), 1) >> lr)

    pieces = {}
    for t in range(n_tiles):
        units = [divmod(t * upt + u, n_heads) for u in range(upt)]

        def rows_of(base):
            return jnp.concatenate([act_ref[sq * r:(sq + 1) * r, base + h * hd:base + (h + 1) * hd]
                                    for sq, h in units], axis=0)

        q = rows_of(0)
        k = rows_of(bw)
        v = rows_of(2 * bw)
        zt = jnp.concatenate([zf[sq * r:(sq + 1) * r, h * hd:(h + 1) * hd] for sq, h in units], axis=0)
        h_row = (unit_row + t * upt) & (n_heads - 1)

        def col(tabs, off):
            tab = jnp.concatenate([tabs[sq] for sq, _ in units], axis=0)
            return jnp.sum(jnp.where(lane == h_row + off, tab, 0.0), axis=1, keepdims=True)

        b_col = col(beta_l, 0)
        gc_col = col(gc_l, n_heads)
        egc_col = col(egc_l, n_heads)
        ekd_col = col(ekd_l, n_heads)
        g_i = jnp.broadcast_to(gc_col, (tr, tr))
        g_j = g_i.T
        dec = jnp.where(incl, jnp.exp(jnp.where(incl, g_i - g_j, 0.0)), 0.0)
        kb = k * b_col
        x = lax.dot_general(jnp.concatenate([kb, q], axis=0).astype(BF16), k.astype(BF16),
                            (((1,), (1,)), ((), ())), preferred_element_type=F32)
        a = jnp.where(strict, x[:tr], 0.0) * dec
        qk = x[tr:] * dec
        tinv = _inv_unit_lower(a, rr, cc, r)
        uw = _dot(tinv, jnp.concatenate([v * b_col, kb * egc_col], axis=1))
        u = uw[:, :hd]
        w = uw[:, hd:]
        qd = q * egc_col
        s_units = [s_ref[sq, h] for sq, h in units]
        sst = jnp.concatenate(s_units, axis=0)
        lhs_y = jnp.concatenate([jnp.where(own_y, jnp.tile(w, (1, upt)), 0.0),
                                 jnp.where(own_y, jnp.tile(qd, (1, upt)), 0.0)], axis=0)
        y = _dot(lhs_y, sst)
        v_new = u - y[:tr]
        kdt = (k * ekd_col).T
        lhs_z = jnp.concatenate([qk, jnp.where(own_z, jnp.tile(kdt, (upt, 1)), 0.0)], axis=0)
        zz = _dot(lhs_z, v_new)
        o = y[tr:] + zz[:tr]
        for ui, (sq, h) in enumerate(units):
            s_ref[sq, h] = (s_units[ui] * egl_l[sq][:, n_heads + h:n_heads + h + 1]
                            + zz[tr + ui * hd:tr + (ui + 1) * hd])
        ms = jnp.mean(o * o, axis=-1, keepdims=True)
        out_t = o * lax.rsqrt(ms + EPS) * onw * zt
        for ui, (sq, h) in enumerate(units):
            pieces[(sq, h)] = out_t[ui * r:(ui + 1) * r]

    for h in range(n_heads):
        blk = pieces[(0, h)] if nsq == 1 else jnp.concatenate([pieces[(sq, h)] for sq in range(nsq)], axis=0)
        o_ref[:, h * hd:(h + 1) * hd] = blk.astype(o_ref.dtype)


def delta_rule(act, zs, bd, s0, alog_row, dtb_row, onw, layer, row0, n_seq, seq):
    _, c3 = act.shape
    bw = c3 // 3
    n_heads = bw // HEAD_DIM
    assert n_heads & (n_heads - 1) == 0 and 2 * n_heads <= LANES
    r = min(seq, DELTA_CHUNK)
    assert r & (r - 1) == 0 and r >= 8 and seq % r == 0
    n_chunks = seq // r
    tile_rows = 256
    nsq = max(1, tile_rows // (n_heads * r)) if n_chunks == 1 else 1
    tr = min(tile_rows, nsq * n_heads * r)
    brows = nsq * r
    assert brows % 16 == 0 and row0 % brows == 0 and n_seq % nsq == 0
    assert nsq == 1 or n_chunks == 1
    r0 = row0 // brows
    hd = HEAD_DIM
    if s0.ndim == 5:
        s0_spec = pl.BlockSpec((None, nsq, n_heads, hd, hd), lambda b, c: (layer, b, 0, 0, 0))
    else:
        s0_spec = pl.BlockSpec((nsq, n_heads, hd, hd), lambda b, c: (b, 0, 0, 0))
    rowmap = lambda b, c: (b * n_chunks + c, 0)
    rowmap_t = lambda b, c: (r0 + b * n_chunks + c, 0)
    return pl.pallas_call(
        functools.partial(_delta_body, n_heads=n_heads, r=r, nsq=nsq, tr=tr),
        out_shape=[jax.ShapeDtypeStruct((n_seq * seq, bw), BF16),
                   jax.ShapeDtypeStruct((n_seq, n_heads, hd, hd), F32)],
        grid=(n_seq // nsq, n_chunks),
        in_specs=[
            pl.BlockSpec((brows, c3), rowmap),
            pl.BlockSpec((brows, bw), rowmap_t),
            pl.BlockSpec((brows, LANES), rowmap_t),
            s0_spec,
            pl.BlockSpec((1, LANES), lambda b, c: (0, 0)),
            pl.BlockSpec((1, LANES), lambda b, c: (0, 0)),
            pl.BlockSpec((1, hd), lambda b, c: (0, 0)),
        ],
        out_specs=[pl.BlockSpec((brows, bw), rowmap),
                   pl.BlockSpec((nsq, n_heads, hd, hd), lambda b, c: (b, 0, 0, 0))],
        compiler_params=_params("arbitrary", "arbitrary"),
        name="delta_rule",
    )(act, zs, bd, s0, alog_row, dtb_row, onw)


def _gate_up_body(se_ref, ss_ref, sf_ref, sa_ref, x_ref, wg_ref, wu_ref, o_ref, wgb, wub):
    i = pl.program_id(1)

    @pl.when(sf_ref[i] == 1)
    def _():
        wgb[...] = wg_ref[...].astype(BF16)
        wub[...] = wu_ref[...].astype(BF16)

    @pl.when(sa_ref[i] == 1)
    def _():
        x = x_ref[...]
        a = jnp.dot(x, wgb[...], preferred_element_type=F32)
        b = jnp.dot(x, wub[...], preferred_element_type=F32)
        o_ref[...] = (_silu(a) * b).astype(o_ref.dtype)

    @pl.when(sa_ref[i] == 0)
    def _():
        o_ref[...] = jnp.zeros_like(o_ref)


def gate_up(xs, wg, wu, widx, plan, tm):
    p, k = xs.shape
    f = wg.shape[-1]
    tn = _tile(f, 512)
    n_tiles = p // tm
    grid_spec = pltpu.PrefetchScalarGridSpec(
        num_scalar_prefetch=4,
        grid=(f // tn, n_tiles),
        in_specs=[
            pl.BlockSpec((tm, k), lambda j, i, se, ss, sf, sa: (ss[i], 0)),
            pl.BlockSpec((None, None, k, tn), lambda j, i, se, ss, sf, sa: (widx, se[i], 0, j)),
            pl.BlockSpec((None, None, k, tn), lambda j, i, se, ss, sf, sa: (widx, se[i], 0, j)),
        ],
        out_specs=pl.BlockSpec((tm, tn), lambda j, i, se, ss, sf, sa: (i, j)),
        scratch_shapes=[pltpu.VMEM((k, tn), BF16), pltpu.VMEM((k, tn), BF16)],
    )
    return pl.pallas_call(
        _gate_up_body,
        out_shape=jax.ShapeDtypeStruct((p, f), BF16),
        grid_spec=grid_spec,
        compiler_params=_params("arbitrary", "arbitrary"),
        name="ffn_gate_up",
    )(plan["ct_expert"], plan["ct_src"], plan["ct_first"], plan["ct_active"], xs, wg, wu)


def _down_body(be_ref, ba_ref, sc_ref, sa_ref, *refs, nk, nsub, sub, residual):
    h_refs = refs[:nsub]
    w_ref = refs[nsub]
    if residual:
        x_ref, o_ref, acc = refs[nsub + 1:]
    else:
        o_ref, acc = refs[nsub + 1:]
    b = pl.program_id(1)
    kk = pl.program_id(2)
    wb = w_ref[...].astype(BF16)
    for s in range(nsub):
        rs = slice(s * sub, (s + 1) * sub)
        active = sa_ref[b * nsub + s] == 1
        h_ref = h_refs[s]

        @pl.when(active)
        def _():
            @pl.when(kk == 0)
            def _():
                acc[rs, :] = jnp.zeros((sub, acc.shape[1]), F32)

            acc[rs, :] += jnp.dot(h_ref[...], wb, preferred_element_type=F32)

            @pl.when(kk == nk - 1)
            def _():
                if residual:
                    o_ref[rs, :] = x_ref[rs, :] + acc[rs, :]
                else:
                    o_ref[rs, :] = acc[rs, :]

        @pl.when(jnp.logical_not(active) & (kk == 0))
        def _():
            o_ref[rs, :] = jnp.zeros((sub, o_ref.shape[1]), o_ref.dtype)


def down(h, wd, widx, plan, big, sub, x_res=None):
    _, f = h.shape
    d = wd.shape[-1]
    tn = _tile(d, 1024)
    tk = _tile(f, 1024)
    nk = f // tk
    nsub = big // sub
    n_big = plan["big_expert"].shape[0]

    def hmap(s):
        return lambda n, b, kk, be, ba, sc, sa: (sc[b * nsub + s], jnp.where(sa[b * nsub + s] == 1, kk, nk - 1))

    in_specs = [pl.BlockSpec((sub, tk), hmap(s)) for s in range(nsub)]
    in_specs.append(pl.BlockSpec((None, None, tk, tn),
                                 lambda n, b, kk, be, ba, sc, sa: (widx, be[b], jnp.where(ba[b] == 1, kk, nk - 1), n)))
    args = [plan["big_expert"], plan["big_active"], plan["sub_cidx"], plan["sub_active"]] + [h] * nsub + [wd]
    if x_res is not None:
        in_specs.append(pl.BlockSpec((big, tn), lambda n, b, kk, be, ba, sc, sa: (b, n)))
        args.append(x_res)
    grid_spec = pltpu.PrefetchScalarGridSpec(
        num_scalar_prefetch=4,
        grid=(d // tn, n_big, nk),
        in_specs=in_specs,
        out_specs=pl.BlockSpec((big, tn), lambda n, b, kk, be, ba, sc, sa: (b, n)),
        scratch_shapes=[pltpu.VMEM((big, tn), F32)],
    )
    return pl.pallas_call(
        functools.partial(_down_body, nk=nk, nsub=nsub, sub=sub, residual=x_res is not None),
        out_shape=jax.ShapeDtypeStruct((n_big * big, d), F32),
        grid_spec=grid_spec,
        compiler_params=_params("arbitrary", "arbitrary", "arbitrary"),
        name="ffn_down",
    )(*args)


def _router_body(x_ref, nw_ref, r_ref, xn_ref, comb_ref, sel_ref, *, n_experts):
    x = x_ref[...]
    ms = jnp.mean(x * x, axis=-1, keepdims=True)
    xn = x * lax.rsqrt(ms + EPS) * nw_ref[...]
    xn_ref[...] = xn
    logits = jnp.dot(xn, r_ref[...], preferred_element_type=F32, precision=lax.Precision.HIGHEST)
    lane = lax.broadcasted_iota(jnp.int32, logits.shape, 1)
    neg = jnp.float32(-jnp.inf)
    lg = jnp.where(lane < n_experts, logits, neg)
    m1 = jnp.max(lg, axis=-1, keepdims=True)
    i1 = jnp.min(jnp.where(lg == m1, lane, LANES), axis=-1, keepdims=True)
    lg2 = jnp.where(lane == i1, neg, lg)
    m2 = jnp.max(lg2, axis=-1, keepdims=True)
    i2 = jnp.min(jnp.where(lg2 == m2, lane, LANES), axis=-1, keepdims=True)
    e2 = jnp.exp(m2 - m1)
    den = 1.0 + e2
    comb_ref[...] = jnp.where(lane == i1, 1.0 / den, 0.0) + jnp.where(lane == i2, e2 / den, 0.0)
    sel_ref[...] = ((lane == i1) | (lane == i2)).astype(jnp.int32)


def router_topk(x, nw, router_pad, n_experts):
    m, d = x.shape
    tm = _tile(m, 256)
    return pl.pallas_call(
        functools.partial(_router_body, n_experts=n_experts),
        out_shape=[jax.ShapeDtypeStruct((m, d), F32),
                   jax.ShapeDtypeStruct((m, LANES), F32),
                   jax.ShapeDtypeStruct((m, LANES), jnp.int32)],
        grid=(m // tm,),
        in_specs=[pl.BlockSpec((tm, d), lambda i: (i, 0)),
                  pl.BlockSpec((1, d), lambda i: (0, 0)),
                  pl.BlockSpec((d, LANES), lambda i: (0, 0))],
        out_specs=[pl.BlockSpec((tm, d), lambda i: (i, 0)),
                   pl.BlockSpec((tm, LANES), lambda i: (i, 0)),
                   pl.BlockSpec((tm, LANES), lambda i: (i, 0))],
        compiler_params=_params("arbitrary"),
        name="router_topk",
    )(x, nw.reshape(1, d), router_pad)


def _row_copy(src_hbm, row, dst, r, sem):
    return pltpu.make_async_copy(src_hbm.at[pl.ds(row, 1), :], dst.at[pl.ds(r, 1), :], sem)


def _gather_body(idx_ref, act_ref, x_hbm, o_ref, buf, sems, *, rows):
    g = pl.program_id(0)
    slot = g % 2

    def issue(blk, sl):
        def start(r, carry):
            _row_copy(x_hbm, idx_ref[blk * rows + r], buf.at[sl], r, sems.at[sl]).start()
            return carry

        lax.fori_loop(0, rows, start, 0)

    @pl.when((g == 0) & (act_ref[0] == 1))
    def _():
        issue(0, 0)

    nxt = jnp.minimum(g + 1, pl.num_programs(0) - 1)

    @pl.when((g + 1 < pl.num_programs(0)) & (act_ref[nxt] == 1))
    def _():
        issue(g + 1, 1 - slot)

    @pl.when(act_ref[g] == 1)
    def _():
        def wait(r, carry):
            _row_copy(x_hbm, 0, buf.at[slot], r, sems.at[slot]).wait()
            return carry

        lax.fori_loop(0, rows, wait, 0)
        o_ref[...] = buf[slot].astype(o_ref.dtype)

    @pl.when(act_ref[g] == 0)
    def _():
        o_ref[...] = jnp.zeros_like(o_ref)


def gather_rows(x, src_rows, blk_active, rows, out_dtype):
    _, d = x.shape
    p = src_rows.shape[0]
    grid_spec = pltpu.PrefetchScalarGridSpec(
        num_scalar_prefetch=2,
        grid=(p // rows,),
        in_specs=[pl.BlockSpec(memory_space=pl.ANY)],
        out_specs=pl.BlockSpec((rows, d), lambda i, idx, act: (i, 0)),
        scratch_shapes=[pltpu.VMEM((2, rows, d), x.dtype), pltpu.SemaphoreType.DMA((2,))],
    )
    return pl.pallas_call(
        functools.partial(_gather_body, rows=rows),
        out_shape=jax.ShapeDtypeStruct((p, d), out_dtype),
        grid_spec=grid_spec,
        compiler_params=_params("arbitrary"),
        name="moe_gather",
    )(src_rows, blk_active, x)


def _combine_body(p0_ref, p1_ref, x_ref, g0_ref, g1_ref, o_hbm, y_ref, buf_a, buf_b, sems, *, rows):
    base = pl.program_id(0) * rows

    def start(r, carry):
        _row_copy(o_hbm, p0_ref[base + r], buf_a, r, sems.at[0]).start()
        _row_copy(o_hbm, p1_ref[base + r], buf_b, r, sems.at[1]).start()
        return carry

    lax.fori_loop(0, rows, start, 0)

    def wait(r, carry):
        _row_copy(o_hbm, 0, buf_a, r, sems.at[0]).wait()
        _row_copy(o_hbm, 0, buf_b, r, sems.at[1]).wait()
        return carry

    lax.fori_loop(0, rows, wait, 0)
    g0 = g0_ref[...]
    g1 = g1_ref[...]
    for cblk in range(x_ref.shape[1] // LANES):
        cs = slice(cblk * LANES, (cblk + 1) * LANES)
        y_ref[:, cs] = x_ref[:, cs] + g0 * buf_a[:, cs] + g1 * buf_b[:, cs]


def combine(x, o_rows, pos0, pos1, g0b, g1b):
    m, d = x.shape
    rows = _tile(m, 256)
    grid_spec = pltpu.PrefetchScalarGridSpec(
        num_scalar_prefetch=2,
        grid=(m // rows,),
        in_specs=[pl.BlockSpec((rows, d), lambda i, a, b: (i, 0)),
                  pl.BlockSpec((rows, LANES), lambda i, a, b: (i, 0)),
                  pl.BlockSpec((rows, LANES), lambda i, a, b: (i, 0)),
                  pl.BlockSpec(memory_space=pl.ANY)],
        out_specs=pl.BlockSpec((rows, d), lambda i, a, b: (i, 0)),
        scratch_shapes=[pltpu.VMEM((rows, d), F32), pltpu.VMEM((rows, d), F32),
                        pltpu.SemaphoreType.DMA((2,))],
    )
    return pl.pallas_call(
        functools.partial(_combine_body, rows=rows),
        out_shape=jax.ShapeDtypeStruct((m, d), F32),
        grid_spec=grid_spec,
        compiler_params=_params("arbitrary"),
        name="moe_combine",
    )(pos0, pos1, x, g0b, g1b, o_rows)


def _group_plan(counts, n_ct, n_big, big, sub):
    i32 = jnp.int32
    nsub = big // sub
    n_experts = counts.shape[0]
    k = (counts + sub - 1) // sub
    c_end = jnp.cumsum(k)
    c_start = c_end - k
    n_ct_active = c_end[-1]
    ct = jnp.arange(n_ct, dtype=i32)
    ct_src = jnp.minimum(ct, n_ct_active - 1).astype(i32)
    ct_expert = jnp.minimum(jnp.searchsorted(c_end, ct_src, side="right"), n_experts - 1).astype(i32)
    prev = jnp.concatenate([jnp.full((1,), -1, i32), ct_expert[:-1]])
    btiles = (counts + big - 1) // big
    big_end = jnp.cumsum(btiles)
    big_start = big_end - btiles
    n_big_active = big_end[-1]
    bt = jnp.arange(n_big, dtype=i32)
    bt_c = jnp.minimum(bt, n_big_active - 1)
    big_expert = jnp.minimum(jnp.searchsorted(big_end, bt_c, side="right"), n_experts - 1).astype(i32)
    st = jnp.arange(n_big * nsub, dtype=i32)
    e_s = big_expert[st // nsub]
    m = (bt_c[st // nsub] - big_start[e_s]) * nsub + st % nsub
    sub_active = ((st // nsub < n_big_active) & (m < k[e_s])).astype(i32)
    sub_cidx = (c_start[e_s] + jnp.minimum(m, k[e_s] - 1)).astype(i32)
    plan = dict(ct_expert=ct_expert, ct_src=ct_src, ct_first=(ct_expert != prev).astype(i32),
                ct_active=(ct < n_ct_active).astype(i32), big_expert=big_expert,
                big_active=(bt < n_big_active).astype(i32), sub_cidx=sub_cidx, sub_active=sub_active)
    return plan, c_start, big_start


def _dispatch_plan(sel, comb, n_experts, big, sub):
    t = sel.shape[0]
    n_ct = (2 * t) // sub + n_experts
    n_big = (2 * t) // big + n_experts
    sel_e = sel[:, :n_experts]
    csum = jnp.cumsum(sel_e, axis=0)
    counts = csum[-1]
    rank = csum - sel_e
    plan, c_start, big_start = _group_plan(counts, n_ct, n_big, big, sub)
    pos_c = c_start[None, :] * sub + rank
    pos_b = big_start[None, :] * big + rank
    c_rows = n_ct * sub
    tok = jnp.broadcast_to(jnp.arange(t, dtype=jnp.int32)[:, None], pos_c.shape)
    flat_pos = jnp.where(sel_e > 0, pos_c, c_rows).reshape(-1)
    src_rows = jnp.zeros((c_rows,), jnp.int32).at[flat_pos].set(tok.reshape(-1), mode="drop")
    far = jnp.int32(2**30)
    lo = jnp.where(sel_e > 0, pos_b, far)
    hi = jnp.where(sel_e > 0, pos_b, -1)
    pos_lo = jnp.min(lo, axis=1).astype(jnp.int32)
    pos_hi = jnp.max(hi, axis=1).astype(jnp.int32)
    comb_e = comb[:, :n_experts]
    g_lo = jnp.take_along_axis(comb_e, jnp.argmin(lo, axis=1)[:, None], axis=1)
    g_hi = jnp.take_along_axis(comb_e, jnp.argmax(hi, axis=1)[:, None], axis=1)
    return plan, src_rows, pos_lo, pos_hi, jnp.broadcast_to(g_lo, (t, LANES)), jnp.broadcast_to(g_hi, (t, LANES))


def _dense_plan(t, big, sub):
    return _group_plan(jnp.full((1,), t, jnp.int32), t // sub, t // big, big, sub)[0]


def kernel(x_prompt, x_sample, state_delta, state_conv, norm_mix, norm_ffn, norm_final, w_in, conv_w, a_log, dt_bias, o_norm_w, sgu_norm_w, w_spatial, b_spatial, w_proj_a, w_proj_b, w_out, ffn_w_gate, ffn_w_up, ffn_w_down, router, moe_w_gate, moe_w_up, moe_w_down):
    bp, lp, d = x_prompt.shape
    bs, ls, _ = x_sample.shape
    depth = norm_mix.shape[0]
    n_groups = w_spatial.shape[1]
    aw = sgu_norm_w.shape[1]
    n_heads = a_log.shape[1]
    bw = n_heads * HEAD_DIM
    qkv_dim = 3 * bw
    n_experts = router.shape[2]
    tp, ts = bp * lp, bs * ls
    t = tp + ts
    c2 = 2 * aw
    c3 = c2 + qkv_dim
    c4 = c3 + bw
    c6 = c4 + 2 * n_heads
    assert lp % CHUNK == 0 and ls <= CHUNK and CHUNK % ls == 0 and ls >= CONV_W - 1
    assert c4 % LANES == 0 and w_in.shape[2] == c6 + 2 * d

    x = jnp.concatenate([x_prompt.reshape(tp, d), x_sample.reshape(ts, d)], axis=0)
    sh = c6 - c4
    w_tail = jnp.pad(w_in[:, :, c4 + 2 * d:], ((0, 0), (0, 0), (0, LANES - sh)))
    zero_conv = jnp.zeros((depth, bp, CONV_W - 1, qkv_dim), F32)
    zero_delta = jnp.zeros((bp, n_heads, HEAD_DIM, HEAD_DIM), F32)
    rep = CHUNK // ls
    eye_rep = jnp.eye(rep, dtype=F32)
    lane_pad = LANES - 2 * n_heads

    tail = jnp.arange(CONV_W - 1, dtype=jnp.int32)
    tail_p = (jnp.arange(bp, dtype=jnp.int32)[:, None] * lp + (lp - (CONV_W - 1)) + tail[None, :]).reshape(-1)
    tail_s = (tp + jnp.arange(bs, dtype=jnp.int32)[:, None] * ls + (ls - (CONV_W - 1)) + tail[None, :]).reshape(-1)
    tm_ffn = _tile(t, 512)
    big_dense = _tile(t, 1024)
    big_moe = 4 * tm_ffn
    new_conv_p, new_conv_s, new_delta_p, new_delta_s, chunk_v = [], [], [], [], []
    for l in range(depth):
        xn = rmsnorm(x, norm_mix[l], BF16)
        u = proj(xn, w_in, l, 0, aw, _gelu, BF16)
        gv = proj(xn, w_in, l, aw, aw, _gelu, F32)
        qkv = proj(xn, w_in, l, c2, qkv_dim, _ident, F32)
        zs = proj(xn, w_in, l, c3, bw, _silu, BF16)
        bd = proj(xn, w_in, l, c4, LANES, _ident, F32, tn_pref=LANES)
        g_main = proj(xn, w_in, l, c4, 2 * d, _sigmoid, BF16)
        g_tail = proj(xn, w_tail, l, 0, LANES, _sigmoid, BF16, tn_pref=LANES)

        w_s = w_spatial[l]
        b_t = b_spatial[l].T
        w_s_small = jnp.einsum("ab,gts->gatbs", eye_rep, w_s[:, :ls, :ls]).reshape(n_groups, CHUNK, CHUNK)
        b_t_small = jnp.tile(b_spatial[l][:, :ls], (1, rep)).T
        (a_p,) = spatial(u, gv, sgu_norm_w[l], w_s, b_t, 0, tp, False)
        a_s, v_s = spatial(u, gv, sgu_norm_w[l], w_s_small, b_t_small, tp, ts, True)
        chunk_v.append(v_s.reshape(bs, ls, aw))

        nb_s = _tile(bs, max(1, 128 // ls))
        act_p = conv_prep(qkv, zero_conv, conv_w, l, 0, bp, lp, 1, bw)
        act_s = conv_prep(qkv, state_conv, conv_w, l, tp, bs, ls, nb_s, bw)
        alog_row = jnp.pad(a_log[l], (n_heads, lane_pad)).reshape(1, LANES)
        dtb_row = jnp.pad(dt_bias[l], (n_heads, lane_pad)).reshape(1, LANES)
        onw = o_norm_w[l].reshape(1, HEAD_DIM)
        b_p, s_p = delta_rule(act_p, zs, bd, zero_delta, alog_row, dtb_row, onw, l, 0, bp, lp)
        b_s, s_s = delta_rule(act_s, zs, bd, state_delta, alog_row, dtb_row, onw, l, tp, bs, ls)
        new_delta_p.append(s_p)
        new_delta_s.append(s_s)
        new_conv_p.append(jnp.take(qkv, tail_p, axis=0).reshape(bp, CONV_W - 1, qkv_dim))
        new_conv_s.append(jnp.take(qkv, tail_s, axis=0).reshape(bs, CONV_W - 1, qkv_dim))

        mg = merge(a_p, a_s, b_p, b_s, w_proj_a, w_proj_b, g_main, g_tail, sh, l)
        x = resproj(mg, w_out, l, x)

        j = l // 2
        if l % 2 == 0:
            xf = rmsnorm(x, norm_ffn[l], BF16)
            plan = _dense_plan(t, big_dense, tm_ffn)
            h = gate_up(xf, ffn_w_gate[:, None], ffn_w_up[:, None], j, plan, tm_ffn)
            x = down(h, ffn_w_down[:, None], j, plan, big_dense, tm_ffn, x_res=x)
        else:
            router_pad = jnp.pad(router[j], ((0, 0), (0, LANES - n_experts)))
            xf32, comb, sel = router_topk(x, norm_ffn[l], router_pad, n_experts)
            plan, src_rows, p0, p1, g0b, g1b = _dispatch_plan(sel, comb, n_experts, big_moe, tm_ffn)
            g_rows = _tile(tm_ffn, 256)
            blk_active = jnp.repeat(plan["ct_active"], tm_ffn // g_rows)
            xs = gather_rows(xf32, src_rows, blk_active, g_rows, BF16)
            h = gate_up(xs, moe_w_gate, moe_w_up, j, plan, tm_ffn)
            o_rows = down(h, moe_w_down, j, plan, big_moe, tm_ffn)
            x = combine(x, o_rows, p0, p1, g0b, g1b)

    y_p, y_s = rmsnorm_split(x, norm_final, tp)
    y_prompt = y_p.reshape(bp, lp, d)
    y_sample = y_s.reshape(bs, ls, d)
    return (y_prompt, y_sample, jnp.stack(new_delta_p), jnp.stack(new_conv_p),
            jnp.stack(new_delta_s), jnp.stack(new_conv_s), jnp.stack(chunk_v))
```

```python
import functools

import jax
import jax.numpy as jnp
from jax import lax
from jax.experimental import pallas as pl
from jax.experimental.pallas import tpu as pltpu

F32 = jnp.float32
BF16 = jnp.bfloat16

HEAD_DIM = 128
CHUNK = 128
DELTA_CHUNK = 64
CONV_W = 4
EPS = 1e-6

LANES = 128
V7X_VMEM_BYTES = 64 * 2**20
VMEM_LIMIT = V7X_VMEM_BYTES - 8 * 2**20


def _params(*sem):
    return pltpu.CompilerParams(dimension_semantics=sem, vmem_limit_bytes=VMEM_LIMIT)


def _tile(n, pref):
    t = pref
    while t > 8 and n % t:
        t //= 2
    assert n % t == 0, (n, pref)
    return t


def _dot(a, b):
    return jnp.dot(a.astype(BF16), b.astype(BF16), preferred_element_type=F32)


def _softplus(x):
    return jnp.maximum(x, 0.0) + jnp.log1p(jnp.exp(-jnp.abs(x)))


def _sigmoid(x):
    return jax.nn.sigmoid(x)


def _silu(x):
    return x * jax.nn.sigmoid(x)


def _gelu(x):
    return jax.nn.gelu(x)


def _ident(x):
    return x


def _rmsnorm_body(x_ref, w_ref, o_ref):
    x = x_ref[...].astype(F32)
    ms = jnp.mean(x * x, axis=-1, keepdims=True)
    o_ref[...] = (x * lax.rsqrt(ms + EPS) * w_ref[...]).astype(o_ref.dtype)


def rmsnorm(x, w, out_dtype):
    m, d = x.shape
    tm = _tile(m, 256)
    return pl.pallas_call(
        _rmsnorm_body,
        out_shape=jax.ShapeDtypeStruct((m, d), out_dtype),
        grid=(m // tm,),
        in_specs=[pl.BlockSpec((tm, d), lambda i: (i, 0)), pl.BlockSpec((1, d), lambda i: (0, 0))],
        out_specs=pl.BlockSpec((tm, d), lambda i: (i, 0)),
        compiler_params=_params("arbitrary"),
        name="rmsnorm",
    )(x, w.reshape(1, d).astype(F32))


def _rmsnorm_split_body(x_ref, w_ref, op_ref, os_ref, *, n_p):
    x = x_ref[...]
    ms = jnp.mean(x * x, axis=-1, keepdims=True)
    y = x * lax.rsqrt(ms + EPS) * w_ref[...]
    i = pl.program_id(0)

    @pl.when(i < n_p)
    def _():
        op_ref[...] = y

    @pl.when(i >= n_p)
    def _():
        os_ref[...] = y


def rmsnorm_split(x, w, tp):
    m, d = x.shape
    ts = m - tp
    tm = _tile(ts, 256)
    assert tp % tm == 0
    n_p = tp // tm
    return pl.pallas_call(
        functools.partial(_rmsnorm_split_body, n_p=n_p),
        out_shape=[jax.ShapeDtypeStruct((tp, d), F32), jax.ShapeDtypeStruct((ts, d), F32)],
        grid=(m // tm,),
        in_specs=[pl.BlockSpec((tm, d), lambda i: (i, 0)), pl.BlockSpec((1, d), lambda i: (0, 0))],
        out_specs=[pl.BlockSpec((tm, d), lambda i: (jnp.minimum(i, n_p - 1), 0)),
                   pl.BlockSpec((tm, d), lambda i: (jnp.maximum(i - n_p, 0), 0))],
        compiler_params=_params("arbitrary"),
        name="rmsnorm_split",
    )(x, w.reshape(1, d).astype(F32))


def _proj_body(x_ref, w_ref, o_ref, *, act):
    acc = jnp.dot(x_ref[...], w_ref[...].astype(BF16), preferred_element_type=F32)
    o_ref[...] = act(acc).astype(o_ref.dtype)


def proj(x, w, layer, col0, n, act, out_dtype, tn_pref=512):
    m, k = x.shape
    tm = _tile(m, 1024)
    tn = _tile(n, tn_pref)
    assert col0 % tn == 0
    off = col0 // tn
    return pl.pallas_call(
        functools.partial(_proj_body, act=act),
        out_shape=jax.ShapeDtypeStruct((m, n), out_dtype),
        grid=(m // tm, n // tn),
        in_specs=[
            pl.BlockSpec((tm, k), lambda i, j: (i, 0)),
            pl.BlockSpec((None, k, tn), lambda i, j: (layer, 0, off + j)),
        ],
        out_specs=pl.BlockSpec((tm, tn), lambda i, j: (i, j)),
        compiler_params=_params("arbitrary", "arbitrary"),
        name="proj",
    )(x, w)


def _merge_body(ap_ref, as_ref, bp_ref, bs_ref, wa_ref, wb_ref, ga_ref, gax_ref, gb_ref, gbx_ref, gt_ref, o_ref,
                *, n_p, sh):
    is_p = pl.program_id(0) < n_p
    last = pl.program_id(1) == pl.num_programs(1) - 1
    a = jnp.where(is_p, ap_ref[...], as_ref[...])
    b = jnp.where(is_p, bp_ref[...], bs_ref[...])
    pa = jnp.dot(a, wa_ref[...].astype(BF16), preferred_element_type=F32)
    pb = jnp.dot(b, wb_ref[...].astype(BF16), preferred_element_type=F32)
    tn = o_ref.shape[1]

    def shifted(main, extra):
        g = jnp.concatenate([main.astype(F32), extra.astype(F32)], axis=1)
        return g[:, sh:sh + tn]

    ga = shifted(ga_ref[...], gax_ref[...])
    gb = shifted(gb_ref[...], jnp.where(last, gt_ref[...], gbx_ref[...]))
    o_ref[...] = (ga * pa + gb * pb).astype(o_ref.dtype)


def merge(a_p, a_s, b_p, b_s, wa, wb, g_main, g_tail, sh, layer):
    tp, ka = a_p.shape
    ts = a_s.shape[0]
    kb = b_p.shape[1]
    d = wa.shape[2]
    tm = _tile(ts, 512)
    assert tp % tm == 0
    n_p, n_s = tp // tm, ts // tm
    tn = _tile(d, 512)
    nb = d // tn
    lb = tn // LANES
    n_lane_blk = 2 * d // LANES
    pmap = lambda i, j: (jnp.minimum(i, n_p - 1), 0)
    smap = lambda i, j: (jnp.maximum(i - n_p, 0), 0)
    return pl.pallas_call(
        functools.partial(_merge_body, n_p=n_p, sh=sh),
        out_shape=jax.ShapeDtypeStruct((tp + ts, d), BF16),
        grid=(n_p + n_s, nb),
        in_specs=[
            pl.BlockSpec((tm, ka), pmap),
            pl.BlockSpec((tm, ka), smap),
            pl.BlockSpec((tm, kb), pmap),
            pl.BlockSpec((tm, kb), smap),
            pl.BlockSpec((None, ka, tn), lambda i, j: (layer, 0, j)),
            pl.BlockSpec((None, kb, tn), lambda i, j: (layer, 0, j)),
            pl.BlockSpec((tm, tn), lambda i, j: (i, j)),
            pl.BlockSpec((tm, LANES), lambda i, j: (i, (j + 1) * lb)),
            pl.BlockSpec((tm, tn), lambda i, j: (i, nb + j)),
            pl.BlockSpec((tm, LANES), lambda i, j: (i, jnp.minimum((nb + j + 1) * lb, n_lane_blk - 1))),
            pl.BlockSpec((tm, LANES), lambda i, j: (i, 0)),
        ],
        out_specs=pl.BlockSpec((tm, tn), lambda i, j: (i, j)),
        compiler_params=_params("arbitrary", "arbitrary"),
        name="merge",
    )(a_p, a_s, b_p, b_s, wa, wb, g_main, g_main, g_main, g_main, g_tail)


def _resproj_body(m_ref, w_ref, x_ref, o_ref):
    acc = jnp.dot(m_ref[...], w_ref[...].astype(BF16), preferred_element_type=F32)
    o_ref[...] = x_ref[...] + acc


def resproj(mg, w, layer, x):
    m, k = mg.shape
    d = w.shape[2]
    tm = _tile(m, 1024)
    tn = _tile(d, 512)
    return pl.pallas_call(
        _resproj_body,
        out_shape=jax.ShapeDtypeStruct((m, d), F32),
        grid=(m // tm, d // tn),
        in_specs=[
            pl.BlockSpec((tm, k), lambda i, j: (i, 0)),
            pl.BlockSpec((None, k, tn), lambda i, j: (layer, 0, j)),
            pl.BlockSpec((tm, tn), lambda i, j: (i, j)),
        ],
        out_specs=pl.BlockSpec((tm, tn), lambda i, j: (i, j)),
        compiler_params=_params("arbitrary", "arbitrary"),
        name="resproj",
    )(mg, w, x)


def _spatial_body(u_ref, gv_ref, nw_ref, w_ref, bt_ref, a_ref, *v_out, n_groups, rows):
    gv = gv_ref[...]
    ms = jnp.mean(gv * gv, axis=-1, keepdims=True)
    v = gv * lax.rsqrt(ms + EPS) * nw_ref[...]
    if v_out:
        v_out[0][...] = v
    gd = gv.shape[1] // n_groups
    rr = lax.broadcasted_iota(jnp.int32, (CHUNK, CHUNK), 0)
    cc = lax.broadcasted_iota(jnp.int32, (CHUNK, CHUNK), 1)
    tril = rr >= cc
    for g in range(n_groups):
        wm = jnp.where(tril, w_ref[g], 0.0).astype(BF16)
        bcol = bt_ref[:, g:g + 1]
        for c in range(rows // CHUNK):
            rs = slice(c * CHUNK, (c + 1) * CHUNK)
            cs = slice(g * gd, (g + 1) * gd)
            s = jnp.dot(wm, v[rs, cs].astype(BF16), preferred_element_type=F32) + bcol
            a_ref[rs, cs] = (u_ref[rs, cs].astype(F32) * s).astype(a_ref.dtype)


def spatial(u, gv, nw, w_s, b_t, row0, nrows, write_v):
    _, aw = u.shape
    n_groups = w_s.shape[0]
    rows = _tile(nrows, 256)
    assert rows % CHUNK == 0 and row0 % rows == 0
    r0 = row0 // rows
    out_shape = [jax.ShapeDtypeStruct((nrows, aw), BF16)]
    out_specs = [pl.BlockSpec((rows, aw), lambda i: (i, 0))]
    if write_v:
        out_shape.append(jax.ShapeDtypeStruct((nrows, aw), F32))
        out_specs.append(pl.BlockSpec((rows, aw), lambda i: (i, 0)))
    return pl.pallas_call(
        functools.partial(_spatial_body, n_groups=n_groups, rows=rows),
        out_shape=out_shape,
        grid=(nrows // rows,),
        in_specs=[
            pl.BlockSpec((rows, aw), lambda i: (r0 + i, 0)),
            pl.BlockSpec((rows, aw), lambda i: (r0 + i, 0)),
            pl.BlockSpec((1, aw), lambda i: (0, 0)),
            pl.BlockSpec((n_groups, CHUNK, CHUNK), lambda i: (0, 0, 0)),
            pl.BlockSpec((CHUNK, n_groups), lambda i: (0, 0)),
        ],
        out_specs=out_specs,
        compiler_params=_params("arbitrary"),
        name="spatial",
    )(u, gv, nw.reshape(1, aw), w_s, b_t)


def _conv_body(x_ref, buf_ref, cw_ref, o_ref, cs_ref, *, nb, seq, bw, tc):
    j = pl.program_id(1)
    is_q = j < bw // tc
    is_v = j >= 2 * (bw // tc)
    scale = jnp.where(is_q, HEAD_DIM ** -0.5, 1.0).astype(F32)
    hd = CONV_W - 1
    cw = cw_ref[...]
    for b in range(nb):
        cs_ref[8 - hd:8, :] = buf_ref[b]
        cs_ref[8:8 + seq, :] = x_ref[b * seq:(b + 1) * seq, :]
        acc = cs_ref[8 - hd:8 - hd + seq, :] * cw[0:1]
        for t in range(1, CONV_W):
            acc = acc + cs_ref[8 - hd + t:8 - hd + t + seq, :] * cw[t:t + 1]
        y = _silu(acc)
        parts = []
        for g in range(tc // HEAD_DIM):
            yg = y[:, g * HEAD_DIM:(g + 1) * HEAD_DIM]
            ss = jnp.sum(yg * yg, axis=-1, keepdims=True)
            parts.append(yg * lax.rsqrt(ss + EPS) * scale)
        yn = jnp.concatenate(parts, axis=1) if len(parts) > 1 else parts[0]
        o_ref[b * seq:(b + 1) * seq, :] = jnp.where(is_v, y, yn)


def conv_prep(qkv, buf, conv_w, layer, row0, n_seq, seq, nb, bw):
    _, c3 = qkv.shape
    tc = _tile(bw, 512)
    rows = nb * seq
    assert row0 % rows == 0 and n_seq % nb == 0
    r0 = row0 // rows
    return pl.pallas_call(
        functools.partial(_conv_body, nb=nb, seq=seq, bw=bw, tc=tc),
        out_shape=jax.ShapeDtypeStruct((n_seq * seq, c3), F32),
        grid=(n_seq // nb, c3 // tc),
        in_specs=[
            pl.BlockSpec((rows, tc), lambda i, j: (r0 + i, j)),
            pl.BlockSpec((None, nb, CONV_W - 1, tc), lambda i, j: (layer, i, 0, j)),
            pl.BlockSpec((None, CONV_W, tc), lambda i, j: (layer, 0, j)),
        ],
        out_specs=pl.BlockSpec((rows, tc), lambda i, j: (i, j)),
        scratch_shapes=[pltpu.VMEM((8 + seq, tc), F32)],
        compiler_params=_params("arbitrary", "arbitrary"),
        name="conv_prep",
    )(qkv, buf, conv_w)


def _inv_unit_lower(a, rr, cc, r):
    tr = a.shape[0]
    eye = (rr == cc).astype(F32)

    def blk(b):
        s = b.bit_length() - 1
        return (rr >> s) == (cc >> s)

    b0 = min(16, r)
    n1 = jnp.where(blk(b0), a, 0.0)
    p = eye - n1
    cur = _dot(n1, n1)
    e = 2
    while 2 * e < b0:
        both = _dot(jnp.concatenate([cur, p], axis=0), cur)
        p = p + both[tr:]
        cur = both[:tr]
        e *= 2
    p = p + _dot(p, cur)
    b = b0
    while b < r:
        m = jnp.where(blk(2 * b) & jnp.logical_not(blk(b)), a, 0.0)
        p = p - _dot(p, _dot(m, p))
        b *= 2
    return p


def _delta_body(act_ref, z_ref, bd_ref, s0_ref, alog_ref, dtb_ref, onw_ref, o_ref, s_ref, *, n_heads, r, nsq, tr):
    hd = HEAD_DIM
    bw = n_heads * hd
    upt = tr // r
    n_tiles = nsq * n_heads // upt
    lr = r.bit_length() - 1

    @pl.when(pl.program_id(1) == 0)
    def _():
        s_ref[...] = s0_ref[...]

    alog = alog_ref[...]
    dtb = dtb_ref[...]
    onw = onw_ref[...]
    zf = z_ref[...].astype(F32)
    rowi = lax.broadcasted_iota(jnp.int32, (r, LANES), 0)
    beta_l, gc_l, egc_l, ekd_l, egl_l = [], [], [], [], []
    for sq in range(nsq):
        bd = bd_ref[sq * r:(sq + 1) * r, :]
        gc = -jnp.exp(alog) * _softplus(bd + dtb)
        sh = 1
        while sh < r:
            gc = gc + jnp.where(rowi >= sh, pltpu.roll(gc, sh, axis=0), 0.0)
            sh *= 2
        gl = gc[r - 1:r, :]
        beta_l.append(_sigmoid(bd))
        gc_l.append(gc)
        egc_l.append(jnp.exp(gc))
        ekd_l.append(jnp.exp(gl - gc))
        egl_l.append(jnp.exp(gl))

    rr = lax.broadcasted_iota(jnp.int32, (tr, tr), 0)
    cc = lax.broadcasted_iota(jnp.int32, (tr, tr), 1)
    same = (rr >> lr) == (cc >> lr)
    incl = same & (rr >= cc)
    strict = same & (rr > cc)
    lane = lax.broadcasted_iota(jnp.int32, (tr, LANES), 1)
    unit_row = lax.broadcasted_iota(jnp.int32, (tr, LANES), 0) >> lr
    own_y = (lax.broadcasted_iota(jnp.int32, (tr, upt * hd), 1) >> 7) == (
        lax.broadcasted_iota(jnp.int32, (tr, upt * hd), 0) >> lr)
    own_z = (lax.broadcasted_iota(jnp.int32, (upt * hd, tr), 0) >> 7) == (
        lax.broadcasted_iota(jnp.int32, (upt * hd, tr), 1) >> lr)

    pieces = {}
    for t in range(n_tiles):
        units = [divmod(t * upt + u, n_heads) for u in range(upt)]

        def rows_of(base):
            return jnp.concatenate([act_ref[sq * r:(sq + 1) * r, base + h * hd:base + (h + 1) * hd]
                                    for sq, h in units], axis=0)

        q = rows_of(0)
        k = rows_of(bw)
        v = rows_of(2 * bw)
        zt = jnp.concatenate([zf[sq * r:(sq + 1) * r, h * hd:(h + 1) * hd] for sq, h in units], axis=0)
        h_row = (unit_row + t * upt) & (n_heads - 1)

        def col(tabs, off):
            tab = jnp.concatenate([tabs[sq] for sq, _ in units], axis=0)
            return jnp.sum(jnp.where(lane == h_row + off, tab, 0.0), axis=1, keepdims=True)

        b_col = col(beta_l, 0)
        gc_col = col(gc_l, n_heads)
        egc_col = col(egc_l, n_heads)
        ekd_col = col(ekd_l, n_heads)
        g_i = jnp.broadcast_to(gc_col, (tr, tr))
        g_j = g_i.T
        dec = jnp.where(incl, jnp.exp(jnp.where(incl, g_i - g_j, 0.0)), 0.0)
        kb = k * b_col
        x = lax.dot_general(jnp.concatenate([kb, q], axis=0).astype(BF16), k.astype(BF16),
                            (((1,), (1,)), ((), ())), preferred_element_type=F32)
        a = jnp.where(strict, x[:tr], 0.0) * dec
        qk = x[tr:] * dec
        tinv = _inv_unit_lower(a, rr, cc, r)
        uw = _dot(tinv, jnp.concatenate([v * b_col, kb * egc_col], axis=1))
        u = uw[:, :hd]
        w = uw[:, hd:]
        qd = q * egc_col
        s_units = [s_ref[sq, h] for sq, h in units]
        sst = jnp.concatenate(s_units, axis=0)
        lhs_y = jnp.concatenate([jnp.where(own_y, jnp.tile(w, (1, upt)), 0.0),
                                 jnp.where(own_y, jnp.tile(qd, (1, upt)), 0.0)], axis=0)
        y = _dot(lhs_y, sst)
        v_new = u - y[:tr]
        kdt = (k * ekd_col).T
        lhs_z = jnp.concatenate([qk, jnp.where(own_z, jnp.tile(kdt, (upt, 1)), 0.0)], axis=0)
        zz = _dot(lhs_z, v_new)
        o = y[tr:] + zz[:tr]
        for ui, (sq, h) in enumerate(units):
            s_ref[sq, h] = (s_units[ui] * egl_l[sq][:, n_heads + h:n_heads + h + 1]
                            + zz[tr + ui * hd:tr + (ui + 1) * hd])
        ms = jnp.mean(o * o, axis=-1, keepdims=True)
        out_t = o * lax.rsqrt(ms + EPS) * onw * zt
        for ui, (sq, h) in enumerate(units):
            pieces[(sq, h)] = out_t[ui * r:(ui + 1) * r]

    for h in range(n_heads):
        blk = pieces[(0, h)] if nsq == 1 else jnp.concatenate([pieces[(sq, h)] for sq in range(nsq)], axis=0)
        o_ref[:, h * hd:(h + 1) * hd] = blk.astype(o_ref.dtype)


def delta_rule(act, zs, bd, s0, alog_row, dtb_row, onw, layer, row0, n_seq, seq):
    _, c3 = act.shape
    bw = c3 // 3
    n_heads = bw // HEAD_DIM
    assert n_heads & (n_heads - 1) == 0 and 2 * n_heads <= LANES
    r = min(seq, DELTA_CHUNK)
    assert r & (r - 1) == 0 and r >= 8 and seq % r == 0
    n_chunks = seq // r
    tile_rows = 256
    nsq = max(1, tile_rows // (n_heads * r)) if n_chunks == 1 else 1
    tr = min(tile_rows, nsq * n_heads * r)
    brows = nsq * r
    assert brows % 16 == 0 and row0 % brows == 0 and n_seq % nsq == 0
    assert nsq == 1 or n_chunks == 1
    r0 = row0 // brows
    hd = HEAD_DIM
    if s0.ndim == 5:
        s0_spec = pl.BlockSpec((None, nsq, n_heads, hd, hd), lambda b, c: (layer, b, 0, 0, 0))
    else:
        s0_spec = pl.BlockSpec((nsq, n_heads, hd, hd), lambda b, c: (b, 0, 0, 0))
    rowmap = lambda b, c: (b * n_chunks + c, 0)
    rowmap_t = lambda b, c: (r0 + b * n_chunks + c, 0)
    return pl.pallas_call(
        functools.partial(_delta_body, n_heads=n_heads, r=r, nsq=nsq, tr=tr),
        out_shape=[jax.ShapeDtypeStruct((n_seq * seq, bw), BF16),
                   jax.ShapeDtypeStruct((n_seq, n_heads, hd, hd), F32)],
        grid=(n_seq // nsq, n_chunks),
        in_specs=[
            pl.BlockSpec((brows, c3), rowmap),
            pl.BlockSpec((brows, bw), rowmap_t),
            pl.BlockSpec((brows, LANES), rowmap_t),
            s0_spec,
            pl.BlockSpec((1, LANES), lambda b, c: (0, 0)),
            pl.BlockSpec((1, LANES), lambda b, c: (0, 0)),
            pl.BlockSpec((1, hd), lambda b, c: (0, 0)),
        ],
        out_specs=[pl.BlockSpec((brows, bw), rowmap),
                   pl.BlockSpec((nsq, n_heads, hd, hd), lambda b, c: (b, 0, 0, 0))],
        compiler_params=_params("arbitrary", "arbitrary"),
        name="delta_rule",
    )(act, zs, bd, s0, alog_row, dtb_row, onw)


def _gate_up_body(ce_ref, cs_ref, cf_ref, ca_ref, cr_ref, cn_ref, nr_ref, x_ref, wg_hbm, wu_hbm, o_ref,
                  wgf, wuf, wgb, wub, sems, *, widx):
    j = pl.program_id(0)
    i = pl.program_id(1)
    tn = o_ref.shape[1]
    n_runs = nr_ref[0]
    seg = j * n_runs + cr_ref[i]
    slot = seg % 2

    def copies(e, jj, sl):
        col = pl.multiple_of(jj * tn, tn)
        return (pltpu.make_async_copy(wg_hbm.at[widx, e, :, pl.ds(col, tn)], wgf.at[sl], sems.at[0, sl]),
                pltpu.make_async_copy(wu_hbm.at[widx, e, :, pl.ds(col, tn)], wuf.at[sl], sems.at[1, sl]))

    @pl.when(cf_ref[i] == 1)
    def _():
        @pl.when(seg == 0)
        def _():
            for c in copies(ce_ref[i], j, slot):
                c.start()

        for c in copies(ce_ref[i], j, slot):
            c.wait()
        j_next = jnp.where(cr_ref[i] == n_runs - 1, j + 1, j)

        @pl.when(j_next < pl.num_programs(0))
        def _():
            for c in copies(cn_ref[i], j_next, 1 - slot):
                c.start()

        wgb[...] = wgf[slot].astype(BF16)
        wub[...] = wuf[slot].astype(BF16)

    @pl.when(ca_ref[i] == 1)
    def _():
        x = x_ref[...]
        a = jnp.dot(x, wgb[...], preferred_element_type=F32)
        b = jnp.dot(x, wub[...], preferred_element_type=F32)
        o_ref[...] = (_silu(a) * b).astype(o_ref.dtype)

    @pl.when(ca_ref[i] == 0)
    def _():
        o_ref[...] = jnp.zeros_like(o_ref)


def gate_up(xs, wg, wu, widx, plan, tm):
    p, k = xs.shape
    f = wg.shape[-1]
    tn = _tile(f, 512)
    n_tiles = p // tm
    grid_spec = pltpu.PrefetchScalarGridSpec(
        num_scalar_prefetch=7,
        grid=(f // tn, n_tiles),
        in_specs=[
            pl.BlockSpec((tm, k), lambda j, i, ce, cs, cf, ca, cr, cn, nr: (cs[i], 0)),
            pl.BlockSpec(memory_space=pl.ANY),
            pl.BlockSpec(memory_space=pl.ANY),
        ],
        out_specs=pl.BlockSpec((tm, tn), lambda j, i, ce, cs, cf, ca, cr, cn, nr: (i, j)),
        scratch_shapes=[pltpu.VMEM((2, k, tn), F32), pltpu.VMEM((2, k, tn), F32),
                        pltpu.VMEM((k, tn), BF16), pltpu.VMEM((k, tn), BF16),
                        pltpu.SemaphoreType.DMA((2, 2))],
    )
    return pl.pallas_call(
        functools.partial(_gate_up_body, widx=widx),
        out_shape=jax.ShapeDtypeStruct((p, f), BF16),
        grid_spec=grid_spec,
        compiler_params=_params("arbitrary", "arbitrary"),
        name="ffn_gate_up",
    )(plan["ct_expert"], plan["ct_src"], plan["ct_first"], plan["ct_active"], plan["ct_run"],
      plan["ct_next_expert"], plan["n_runs"], xs, wg, wu)


def _down_body(be_ref, ba_ref, sc_ref, sa_ref, *refs, nk, nsub, sub, residual):
    h_refs = refs[:nsub]
    w_ref = refs[nsub]
    if residual:
        x_ref, o_ref, acc = refs[nsub + 1:]
    else:
        o_ref, acc = refs[nsub + 1:]
    b = pl.program_id(1)
    kk = pl.program_id(2)
    wb = w_ref[...].astype(BF16)
    for s in range(nsub):
        rs = slice(s * sub, (s + 1) * sub)
        active = sa_ref[b * nsub + s] == 1
        h_ref = h_refs[s]

        @pl.when(active)
        def _():
            @pl.when(kk == 0)
            def _():
                acc[rs, :] = jnp.zeros((sub, acc.shape[1]), F32)

            acc[rs, :] += jnp.dot(h_ref[...], wb, preferred_element_type=F32)

            @pl.when(kk == nk - 1)
            def _():
                if residual:
                    o_ref[rs, :] = x_ref[rs, :] + acc[rs, :]
                else:
                    o_ref[rs, :] = acc[rs, :]

        @pl.when(jnp.logical_not(active) & (kk == 0))
        def _():
            o_ref[rs, :] = jnp.zeros((sub, o_ref.shape[1]), o_ref.dtype)


def down(h, wd, widx, plan, big, sub, x_res=None):
    _, f = h.shape
    d = wd.shape[-1]
    tn = _tile(d, 1024)
    tk = _tile(f, 1024)
    nk = f // tk
    nsub = big // sub
    n_big = plan["big_expert"].shape[0]

    def hmap(s):
        return lambda n, b, kk, be, ba, sc, sa: (sc[b * nsub + s], jnp.where(sa[b * nsub + s] == 1, kk, nk - 1))

    in_specs = [pl.BlockSpec((sub, tk), hmap(s)) for s in range(nsub)]
    in_specs.append(pl.BlockSpec((None, None, tk, tn),
                                 lambda n, b, kk, be, ba, sc, sa: (widx, be[b], jnp.where(ba[b] == 1, kk, nk - 1), n)))
    args = [plan["big_expert"], plan["big_active"], plan["sub_cidx"], plan["sub_active"]] + [h] * nsub + [wd]
    if x_res is not None:
        in_specs.append(pl.BlockSpec((big, tn), lambda n, b, kk, be, ba, sc, sa: (b, n)))
        args.append(x_res)
    grid_spec = pltpu.PrefetchScalarGridSpec(
        num_scalar_prefetch=4,
        grid=(d // tn, n_big, nk),
        in_specs=in_specs,
        out_specs=pl.BlockSpec((big, tn), lambda n, b, kk, be, ba, sc, sa: (b, n)),
        scratch_shapes=[pltpu.VMEM((big, tn), F32)],
    )
    return pl.pallas_call(
        functools.partial(_down_body, nk=nk, nsub=nsub, sub=sub, residual=x_res is not None),
        out_shape=jax.ShapeDtypeStruct((n_big * big, d), F32),
        grid_spec=grid_spec,
        compiler_params=_params("arbitrary", "arbitrary", "arbitrary"),
        name="ffn_down",
    )(*args)


def _router_body(x_ref, nw_ref, r_ref, xn_ref, comb_ref, sel_ref, *, n_experts):
    x = x_ref[...]
    ms = jnp.mean(x * x, axis=-1, keepdims=True)
    xn = x * lax.rsqrt(ms + EPS) * nw_ref[...]
    xn_ref[...] = xn
    logits = jnp.dot(xn, r_ref[...], preferred_element_type=F32, precision=lax.Precision.HIGHEST)
    lane = lax.broadcasted_iota(jnp.int32, logits.shape, 1)
    neg = jnp.float32(-jnp.inf)
    lg = jnp.where(lane < n_experts, logits, neg)
    m1 = jnp.max(lg, axis=-1, keepdims=True)
    i1 = jnp.min(jnp.where(lg == m1, lane, LANES), axis=-1, keepdims=True)
    lg2 = jnp.where(lane == i1, neg, lg)
    m2 = jnp.max(lg2, axis=-1, keepdims=True)
    i2 = jnp.min(jnp.where(lg2 == m2, lane, LANES), axis=-1, keepdims=True)
    e2 = jnp.exp(m2 - m1)
    den = 1.0 + e2
    comb_ref[...] = jnp.where(lane == i1, 1.0 / den, 0.0) + jnp.where(lane == i2, e2 / den, 0.0)
    sel_ref[...] = ((lane == i1) | (lane == i2)).astype(jnp.int32)


def router_topk(x, nw, router_pad, n_experts):
    m, d = x.shape
    tm = _tile(m, 256)
    return pl.pallas_call(
        functools.partial(_router_body, n_experts=n_experts),
        out_shape=[jax.ShapeDtypeStruct((m, d), F32),
                   jax.ShapeDtypeStruct((m, LANES), F32),
                   jax.ShapeDtypeStruct((m, LANES), jnp.int32)],
        grid=(m // tm,),
        in_specs=[pl.BlockSpec((tm, d), lambda i: (i, 0)),
                  pl.BlockSpec((1, d), lambda i: (0, 0)),
                  pl.BlockSpec((d, LANES), lambda i: (0, 0))],
        out_specs=[pl.BlockSpec((tm, d), lambda i: (i, 0)),
                   pl.BlockSpec((tm, LANES), lambda i: (i, 0)),
                   pl.BlockSpec((tm, LANES), lambda i: (i, 0))],
        compiler_params=_params("arbitrary"),
        name="router_topk",
    )(x, nw.reshape(1, d), router_pad)


def _row_copy(src_hbm, row, dst, r, sem):
    return pltpu.make_async_copy(src_hbm.at[pl.ds(row, 1), :], dst.at[pl.ds(r, 1), :], sem)


def _gather_body(idx_ref, act_ref, x_hbm, o_ref, buf, sems, *, rows):
    g = pl.program_id(0)
    slot = g % 2

    def issue(blk, sl):
        def start(r, carry):
            _row_copy(x_hbm, idx_ref[blk * rows + r], buf.at[sl], r, sems.at[sl]).start()
            return carry

        lax.fori_loop(0, rows, start, 0)

    @pl.when((g == 0) & (act_ref[0] == 1))
    def _():
        issue(0, 0)

    nxt = jnp.minimum(g + 1, pl.num_programs(0) - 1)

    @pl.when((g + 1 < pl.num_programs(0)) & (act_ref[nxt] == 1))
    def _():
        issue(g + 1, 1 - slot)

    @pl.when(act_ref[g] == 1)
    def _():
        def wait(r, carry):
            _row_copy(x_hbm, 0, buf.at[slot], r, sems.at[slot]).wait()
            return carry

        lax.fori_loop(0, rows, wait, 0)
        o_ref[...] = buf[slot].astype(o_ref.dtype)

    @pl.when(act_ref[g] == 0)
    def _():
        o_ref[...] = jnp.zeros_like(o_ref)


def gather_rows(x, src_rows, blk_active, rows, out_dtype):
    _, d = x.shape
    p = src_rows.shape[0]
    grid_spec = pltpu.PrefetchScalarGridSpec(
        num_scalar_prefetch=2,
        grid=(p // rows,),
        in_specs=[pl.BlockSpec(memory_space=pl.ANY)],
        out_specs=pl.BlockSpec((rows, d), lambda i, idx, act: (i, 0)),
        scratch_shapes=[pltpu.VMEM((2, rows, d), x.dtype), pltpu.SemaphoreType.DMA((2,))],
    )
    return pl.pallas_call(
        functools.partial(_gather_body, rows=rows),
        out_shape=jax.ShapeDtypeStruct((p, d), out_dtype),
        grid_spec=grid_spec,
        compiler_params=_params("arbitrary"),
        name="moe_gather",
    )(src_rows, blk_active, x)


def _combine_body(p0_ref, p1_ref, x_ref, g0_ref, g1_ref, o_hbm, y_ref, buf_a, buf_b, sems, *, rows):
    base = pl.program_id(0) * rows

    def start(r, carry):
        _row_copy(o_hbm, p0_ref[base + r], buf_a, r, sems.at[0]).start()
        _row_copy(o_hbm, p1_ref[base + r], buf_b, r, sems.at[1]).start()
        return carry

    lax.fori_loop(0, rows, start, 0)

    def wait(r, carry):
        _row_copy(o_hbm, 0, buf_a, r, sems.at[0]).wait()
        _row_copy(o_hbm, 0, buf_b, r, sems.at[1]).wait()
        return carry

    lax.fori_loop(0, rows, wait, 0)
    g0 = g0_ref[...]
    g1 = g1_ref[...]
    for cblk in range(x_ref.shape[1] // LANES):
        cs = slice(cblk * LANES, (cblk + 1) * LANES)
        y_ref[:, cs] = x_ref[:, cs] + g0 * buf_a[:, cs] + g1 * buf_b[:, cs]


def combine(x, o_rows, pos0, pos1, g0b, g1b):
    m, d = x.shape
    rows = _tile(m, 256)
    grid_spec = pltpu.PrefetchScalarGridSpec(
        num_scalar_prefetch=2,
        grid=(m // rows,),
        in_specs=[pl.BlockSpec((rows, d), lambda i, a, b: (i, 0)),
                  pl.BlockSpec((rows, LANES), lambda i, a, b: (i, 0)),
                  pl.BlockSpec((rows, LANES), lambda i, a, b: (i, 0)),
                  pl.BlockSpec(memory_space=pl.ANY)],
        out_specs=pl.BlockSpec((rows, d), lambda i, a, b: (i, 0)),
        scratch_shapes=[pltpu.VMEM((rows, d), F32), pltpu.VMEM((rows, d), F32),
                        pltpu.SemaphoreType.DMA((2,))],
    )
    return pl.pallas_call(
        functools.partial(_combine_body, rows=rows),
        out_shape=jax.ShapeDtypeStruct((m, d), F32),
        grid_spec=grid_spec,
        compiler_params=_params("arbitrary"),
        name="moe_combine",
    )(pos0, pos1, x, g0b, g1b, o_rows)


def _group_plan(counts, n_ct, n_big, big, sub):
    i32 = jnp.int32
    nsub = big // sub
    n_experts = counts.shape[0]
    k = (counts + sub - 1) // sub
    c_end = jnp.cumsum(k)
    c_start = c_end - k
    n_ct_active = c_end[-1]
    ct = jnp.arange(n_ct, dtype=i32)
    ct_src = jnp.minimum(ct, n_ct_active - 1).astype(i32)
    ct_expert = jnp.minimum(jnp.searchsorted(c_end, ct_src, side="right"), n_experts - 1).astype(i32)
    prev = jnp.concatenate([jnp.full((1,), -1, i32), ct_expert[:-1]])
    btiles = (counts + big - 1) // big
    big_end = jnp.cumsum(btiles)
    big_start = big_end - btiles
    n_big_active = big_end[-1]
    bt = jnp.arange(n_big, dtype=i32)
    bt_c = jnp.minimum(bt, n_big_active - 1)
    big_expert = jnp.minimum(jnp.searchsorted(big_end, bt_c, side="right"), n_experts - 1).astype(i32)
    st = jnp.arange(n_big * nsub, dtype=i32)
    e_s = big_expert[st // nsub]
    m = (bt_c[st // nsub] - big_start[e_s]) * nsub + st % nsub
    sub_active = ((st // nsub < n_big_active) & (m < k[e_s])).astype(i32)
    sub_cidx = (c_start[e_s] + jnp.minimum(m, k[e_s] - 1)).astype(i32)
    ct_first = (ct_expert != prev).astype(i32)
    ct_run = (jnp.cumsum(ct_first) - 1).astype(i32)
    n_runs = jnp.sum(ct_first).astype(i32)
    run_expert = jnp.zeros((n_experts,), i32).at[jnp.where(ct_first == 1, ct_run, n_experts)].set(ct_expert, mode="drop")
    ct_next_expert = run_expert[(ct_run + 1) % n_runs]
    plan = dict(ct_expert=ct_expert, ct_src=ct_src, ct_first=ct_first, ct_run=ct_run, ct_next_expert=ct_next_expert,
                n_runs=n_runs.reshape(1), ct_active=(ct < n_ct_active).astype(i32), big_expert=big_expert,
                big_active=(bt < n_big_active).astype(i32), sub_cidx=sub_cidx, sub_active=sub_active)
    return plan, c_start, big_start


def _dispatch_plan(sel, comb, n_experts, big, sub):
    t = sel.shape[0]
    n_ct = (2 * t) // sub + n_experts
    n_big = (2 * t) // big + n_experts
    sel_e = sel[:, :n_experts]
    csum = jnp.cumsum(sel_e, axis=0)
    counts = csum[-1]
    rank = csum - sel_e
    plan, c_start, big_start = _group_plan(counts, n_ct, n_big, big, sub)
    pos_c = c_start[None, :] * sub + rank
    pos_b = big_start[None, :] * big + rank
    c_rows = n_ct * sub
    tok = jnp.broadcast_to(jnp.arange(t, dtype=jnp.int32)[:, None], pos_c.shape)
    flat_pos = jnp.where(sel_e > 0, pos_c, c_rows).reshape(-1)
    src_rows = jnp.zeros((c_rows,), jnp.int32).at[flat_pos].set(tok.reshape(-1), mode="drop")
    far = jnp.int32(2**30)
    lo = jnp.where(sel_e > 0, pos_b, far)
    hi = jnp.where(sel_e > 0, pos_b, -1)
    pos_lo = jnp.min(lo, axis=1).astype(jnp.int32)
    pos_hi = jnp.max(hi, axis=1).astype(jnp.int32)
    comb_e = comb[:, :n_experts]
    g_lo = jnp.take_along_axis(comb_e, jnp.argmin(lo, axis=1)[:, None], axis=1)
    g_hi = jnp.take_along_axis(comb_e, jnp.argmax(hi, axis=1)[:, None], axis=1)
    return plan, src_rows, pos_lo, pos_hi, jnp.broadcast_to(g_lo, (t, LANES)), jnp.broadcast_to(g_hi, (t, LANES))


def _dense_plan(t, big, sub):
    return _group_plan(jnp.full((1,), t, jnp.int32), t // sub, t // big, big, sub)[0]


def kernel(x_prompt, x_sample, state_delta, state_conv, norm_mix, norm_ffn, norm_final, w_in, conv_w, a_log, dt_bias, o_norm_w, sgu_norm_w, w_spatial, b_spatial, w_proj_a, w_proj_b, w_out, ffn_w_gate, ffn_w_up, ffn_w_down, router, moe_w_gate, moe_w_up, moe_w_down):
    bp, lp, d = x_prompt.shape
    bs, ls, _ = x_sample.shape
    depth = norm_mix.shape[0]
    n_groups = w_spatial.shape[1]
    aw = sgu_norm_w.shape[1]
    n_heads = a_log.shape[1]
    bw = n_heads * HEAD_DIM
    qkv_dim = 3 * bw
    n_experts = router.shape[2]
    tp, ts = bp * lp, bs * ls
    t = tp + ts
    c2 = 2 * aw
    c3 = c2 + qkv_dim
    c4 = c3 + bw
    c6 = c4 + 2 * n_heads
    assert lp % CHUNK == 0 and ls <= CHUNK and CHUNK % ls == 0 and ls >= CONV_W - 1
    assert c4 % LANES == 0 and w_in.shape[2] == c6 + 2 * d

    x = jnp.concatenate([x_prompt.reshape(tp, d), x_sample.reshape(ts, d)], axis=0)
    sh = c6 - c4
    w_tail = jnp.pad(w_in[:, :, c4 + 2 * d:], ((0, 0), (0, 0), (0, LANES - sh)))
    zero_conv = jnp.zeros((depth, bp, CONV_W - 1, qkv_dim), F32)
    zero_delta = jnp.zeros((bp, n_heads, HEAD_DIM, HEAD_DIM), F32)
    rep = CHUNK // ls
    eye_rep = jnp.eye(rep, dtype=F32)
    lane_pad = LANES - 2 * n_heads

    tail = jnp.arange(CONV_W - 1, dtype=jnp.int32)
    tail_p = (jnp.arange(bp, dtype=jnp.int32)[:, None] * lp + (lp - (CONV_W - 1)) + tail[None, :]).reshape(-1)
    tail_s = (tp + jnp.arange(bs, dtype=jnp.int32)[:, None] * ls + (ls - (CONV_W - 1)) + tail[None, :]).reshape(-1)
    tm_ffn = _tile(t, 512)
    big_dense = _tile(t, 1024)
    big_moe = 4 * tm_ffn
    new_conv_p, new_conv_s, new_delta_p, new_delta_s, chunk_v = [], [], [], [], []
    for l in range(depth):
        xn = rmsnorm(x, norm_mix[l], BF16)
        u = proj(xn, w_in, l, 0, aw, _gelu, BF16)
        gv = proj(xn, w_in, l, aw, aw, _gelu, F32)
        qkv = proj(xn, w_in, l, c2, qkv_dim, _ident, F32)
        zs = proj(xn, w_in, l, c3, bw, _silu, BF16)
        bd = proj(xn, w_in, l, c4, LANES, _ident, F32, tn_pref=LANES)
        g_main = proj(xn, w_in, l, c4, 2 * d, _sigmoid, BF16)
        g_tail = proj(xn, w_tail, l, 0, LANES, _sigmoid, BF16, tn_pref=LANES)

        w_s = w_spatial[l]
        b_t = b_spatial[l].T
        w_s_small = jnp.einsum("ab,gts->gatbs", eye_rep, w_s[:, :ls, :ls]).reshape(n_groups, CHUNK, CHUNK)
        b_t_small = jnp.tile(b_spatial[l][:, :ls], (1, rep)).T
        (a_p,) = spatial(u, gv, sgu_norm_w[l], w_s, b_t, 0, tp, False)
        a_s, v_s = spatial(u, gv, sgu_norm_w[l], w_s_small, b_t_small, tp, ts, True)
        chunk_v.append(v_s.reshape(bs, ls, aw))

        nb_s = _tile(bs, max(1, 128 // ls))
        act_p = conv_prep(qkv, zero_conv, conv_w, l, 0, bp, lp, 1, bw)
        act_s = conv_prep(qkv, state_conv, conv_w, l, tp, bs, ls, nb_s, bw)
        alog_row = jnp.pad(a_log[l], (n_heads, lane_pad)).reshape(1, LANES)
        dtb_row = jnp.pad(dt_bias[l], (n_heads, lane_pad)).reshape(1, LANES)
        onw = o_norm_w[l].reshape(1, HEAD_DIM)
        b_p, s_p = delta_rule(act_p, zs, bd, zero_delta, alog_row, dtb_row, onw, l, 0, bp, lp)
        b_s, s_s = delta_rule(act_s, zs, bd, state_delta, alog_row, dtb_row, onw, l, tp, bs, ls)
        new_delta_p.append(s_p)
        new_delta_s.append(s_s)
        new_conv_p.append(jnp.take(qkv, tail_p, axis=0).reshape(bp, CONV_W - 1, qkv_dim))
        new_conv_s.append(jnp.take(qkv, tail_s, axis=0).reshape(bs, CONV_W - 1, qkv_dim))

        mg = merge(a_p, a_s, b_p, b_s, w_proj_a, w_proj_b, g_main, g_tail, sh, l)
        x = resproj(mg, w_out, l, x)

        j = l // 2
        if l % 2 == 0:
            xf = rmsnorm(x, norm_ffn[l], BF16)
            plan = _dense_plan(t, big_dense, tm_ffn)
            h = gate_up(xf, ffn_w_gate[:, None], ffn_w_up[:, None], j, plan, tm_ffn)
            x = down(h, ffn_w_down[:, None], j, plan, big_dense, tm_ffn, x_res=x)
        else:
            router_pad = jnp.pad(router[j], ((0, 0), (0, LANES - n_experts)))
            xf32, comb, sel = router_topk(x, norm_ffn[l], router_pad, n_experts)
            plan, src_rows, p0, p1, g0b, g1b = _dispatch_plan(sel, comb, n_experts, big_moe, tm_ffn)
            g_rows = _tile(tm_ffn, 256)
            blk_active = jnp.repeat(plan["ct_active"], tm_ffn // g_rows)
            xs = gather_rows(xf32, src_rows, blk_active, g_rows, BF16)
            h = gate_up(xs, moe_w_gate, moe_w_up, j, plan, tm_ffn)
            o_rows = down(h, moe_w_down, j, plan, big_moe, tm_ffn)
            x = combine(x, o_rows, p0, p1, g0b, g1b)

    y_p, y_s = rmsnorm_split(x, norm_final, tp)
    y_prompt = y_p.reshape(bp, lp, d)
    y_sample = y_s.reshape(bs, ls, d)
    return (y_prompt, y_sample, jnp.stack(new_delta_p), jnp.stack(new_conv_p),
            jnp.stack(new_delta_s), jnp.stack(new_conv_s), jnp.stack(chunk_v))
```
